```python
import jax, jax.numpy as jnp
from jax import lax
import numpy as np

D_MODEL = 1024
BATCH = 2
SEQ = 8192
DEPTH = 4
DEC_BATCH = 128
DEC_SEQ = 1
PAST_LEN = 8192
PAGE_SIZE = 128

N_EVEN = (DEPTH + 1) // 2
N_ODD = DEPTH // 2
CHUNK = 128
A_WIDTH = D_MODEL
A_GROUPS = 4
B_WIDTH = D_MODEL
CONV_WIDTH = 31
MLA_HEADS = 8
Q_LORA = D_MODEL // 2
KV_LORA = D_MODEL // 4
QK_NOPE = 128
QK_ROPE = 64
V_HEAD = 128
ROPE_THETA = 10000.0
ATTN_SCALE = (QK_NOPE + QK_ROPE) ** -0.5
FFN_HIDDEN = ((8 * D_MODEL // 3 + 255) // 256) * 256
EPS = 1e-6

kernel_name = "hybrid_gmlp_conformer_mla_decode_step"


def rmsnorm(x, g):
    xf = x.astype(jnp.float32)
    y = xf * lax.rsqrt(jnp.mean(xf * xf, axis=-1, keepdims=True) + EPS)
    return (y * g.astype(jnp.float32)).astype(x.dtype)


def layernorm(x, g, b):
    xf = x.astype(jnp.float32)
    mu = jnp.mean(xf, axis=-1, keepdims=True)
    var = jnp.mean(jnp.square(xf - mu), axis=-1, keepdims=True)
    y = (xf - mu) * lax.rsqrt(var + EPS)
    return (y * g.astype(jnp.float32) + b.astype(jnp.float32)).astype(x.dtype)


def rope(x, pos):
    half = QK_ROPE // 2
    inv = ROPE_THETA ** (-jnp.arange(half, dtype=jnp.float32) / half)
    ang = pos.astype(jnp.float32)[:, None] * inv[None, :]
    shape = (1, pos.shape[0]) + (1,) * (x.ndim - 3) + (half,)
    c = jnp.cos(ang).reshape(shape)
    s = jnp.sin(ang).reshape(shape)
    xf = x.astype(jnp.float32)
    x1, x2 = xf[..., :half], xf[..., half:]
    return jnp.concatenate([x1 * c - x2 * s, x1 * s + x2 * c], axis=-1).astype(x.dtype)


def chunk_spatial_gate(v, w_s, b_s):
    bsz, t, _ = v.shape
    length = min(t, CHUNK)
    n = -(-t // length)
    pad = n * length - t
    vp = jnp.pad(v, ((0, 0), (0, pad), (0, 0))) if pad else v
    vp = vp.reshape(bsz, n, length, A_GROUPS, A_WIDTH // A_GROUPS)
    mask = jnp.tril(jnp.ones((length, length), dtype=bool))
    w = jnp.where(mask[None], w_s[:, :length, :length], 0.0).astype(v.dtype)
    s = jnp.einsum('gij,bnjgc->bnigc', w, vp) + b_s[:, :length].T[None, None, :, :, None]
    return s.reshape(bsz, n * length, A_WIDTH)[:, :t]


def causal_depthwise_conv(x, prefix, w, b):
    xe = jnp.concatenate([prefix, x], axis=1)
    y = lax.conv_general_dilated(xe, w[:, None, :].astype(xe.dtype), window_strides=(1,), padding='VALID',
                                 dimension_numbers=('NWC', 'WIO', 'NWC'),
                                 feature_group_count=x.shape[-1])
    return y + b, xe[:, -(CONV_WIDTH - 1):]


def even_mixer(h, conv_prefix, w_in, ln_v_g, ln_v_b, w_s, b_s, conv_w, conv_b, ln_c_g, ln_c_b, w_out):
    z = h @ w_in
    zu, zv, zb, zg = jnp.split(z, [A_WIDTH, 2 * A_WIDTH, 2 * A_WIDTH + B_WIDTH], axis=-1)
    u = jax.nn.gelu(zu)
    v = layernorm(jax.nn.gelu(zv), ln_v_g, ln_v_b)
    a_out = u * chunk_spatial_gate(v, w_s, b_s)
    glu = zb * jax.nn.sigmoid(zg)
    c, new_buf = causal_depthwise_conv(glu, conv_prefix, conv_w, conv_b)
    b_out = jax.nn.silu(layernorm(c, ln_c_g, ln_c_b))
    y = jnp.concatenate([a_out, b_out], axis=-1) @ w_out
    return y, new_buf, v


def mla_project(h, pos, w_in_c, q_norm, kv_norm, w_uq):
    bsz, t, _ = h.shape
    z = h @ w_in_c
    cq, ckv, kpe = jnp.split(z, [Q_LORA, Q_LORA + KV_LORA], axis=-1)
    cq = rmsnorm(cq, q_norm)
    ckv = rmsnorm(ckv, kv_norm)
    kpe = rope(kpe, pos)
    q = (cq @ w_uq).reshape(bsz, t, MLA_HEADS, QK_NOPE + QK_ROPE)
    q_nope = q[..., :QK_NOPE]
    q_pe = rope(q[..., QK_NOPE:], pos)
    return q_nope, q_pe, ckv, kpe


def mla_prompt(h, w_in_c, q_norm, kv_norm, w_uq, w_uk, w_uv, w_out_c):
    bsz, t, _ = h.shape
    pos = jnp.arange(t)
    q_nope, q_pe, ckv, kpe = mla_project(h, pos, w_in_c, q_norm, kv_norm, w_uq)
    k_nope = jnp.einsum('btr,rhn->bthn', ckv, w_uk)
    v = jnp.einsum('btr,rhv->bthv', ckv, w_uv)
    qb = min(t, CHUNK)
    nb = t // qb
    qn = q_nope.reshape(bsz, nb, qb, MLA_HEADS, QK_NOPE).transpose(1, 0, 2, 3, 4)
    qp = q_pe.reshape(bsz, nb, qb, MLA_HEADS, QK_ROPE).transpose(1, 0, 2, 3, 4)
    kpos = jnp.arange(t)

    def block(args):
        qn_b, qp_b, start = args
        s = (jnp.einsum('bqhn,bkhn->bhqk', qn_b, k_nope)
             + jnp.einsum('bqhp,bkp->bhqk', qp_b, kpe)).astype(jnp.float32) * ATTN_SCALE
        qpos = start + jnp.arange(qb)
        s = jnp.where(kpos[None, :] <= qpos[:, None], s, -jnp.inf)
        p = jax.nn.softmax(s, axis=-1).astype(v.dtype)
        return jnp.einsum('bhqk,bkhv->bqhv', p, v)

    o = lax.map(block, (qn, qp, jnp.arange(nb) * qb))
    o = o.transpose(1, 0, 2, 3, 4).reshape(bsz, t, MLA_HEADS * V_HEAD)
    return o @ w_out_c, ckv, kpe


def mla_sample(h, ckv_pool, kpe_pool, layer, page_table, w_in_c, q_norm, kv_norm, w_uq, w_uk, w_uv, w_out_c):
    bsz, s_len, _ = h.shape
    past = page_table.shape[1] * ckv_pool.shape[2]
    pos = past + jnp.arange(s_len)
    q_nope, q_pe, ckv, kpe = mla_project(h, pos, w_in_c, q_norm, kv_norm, w_uq)
    q_lat = jnp.einsum('bshn,rhn->bshr', q_nope, w_uk)
    ckv_past = ckv_pool[layer, page_table].reshape(bsz, past, KV_LORA)
    kpe_past = kpe_pool[layer, page_table].reshape(bsz, past, QK_ROPE)
    s_past = (jnp.einsum('bshr,bkr->bhsk', q_lat, ckv_past)
              + jnp.einsum('bshp,bkp->bhsk', q_pe, kpe_past)).astype(jnp.float32) * ATTN_SCALE
    s_new = (jnp.einsum('bshr,bkr->bhsk', q_lat, ckv)
             + jnp.einsum('bshp,bkp->bhsk', q_pe, kpe)).astype(jnp.float32) * ATTN_SCALE
    causal = jnp.arange(s_len)[None, :] <= jnp.arange(s_len)[:, None]
    s = jnp.concatenate([s_past, jnp.where(causal, s_new, -jnp.inf)], axis=-1)
    p = jax.nn.softmax(s, axis=-1).astype(ckv.dtype)
    o_lat = (jnp.einsum('bhsk,bkr->bshr', p[..., :past], ckv_past)
             + jnp.einsum('bhsk,bkr->bshr', p[..., past:], ckv))
    o = jnp.einsum('bshr,rhv->bshv', o_lat, w_uv).reshape(bsz, s_len, MLA_HEADS * V_HEAD)
    return o @ w_out_c, ckv, kpe


def swiglu(h, wg, wu, wd):
    return (jax.nn.silu(h @ wg) * (h @ wu)) @ wd


def setup_inputs(seed: int = 0) -> dict:
    key = jax.random.key(seed)
    ks = jax.random.split(key, 32)

    def nrm(k, shape, scale):
        return jax.random.normal(k, shape, jnp.float32) * scale

    n_pages = PAST_LEN // PAGE_SIZE
    n_used = DEC_BATCH * n_pages
    n_pool = n_used + n_used // 4
    page_table = jax.random.permutation(ks[5], n_pool)[:n_used].reshape(DEC_BATCH, n_pages).astype(jnp.int32)
    ab_in = 2 * A_WIDTH + 2 * B_WIDTH
    c_in = Q_LORA + KV_LORA + QK_ROPE
    return {
        "x_prompt": nrm(ks[0], (BATCH, SEQ, D_MODEL), 1.0),
        "x_sample": nrm(ks[1], (DEC_BATCH, DEC_SEQ, D_MODEL), 1.0),
        "cache_ckv": nrm(ks[2], (N_ODD, n_pool, PAGE_SIZE, KV_LORA), 1.0),
        "cache_kpe": nrm(ks[3], (N_ODD, n_pool, PAGE_SIZE, QK_ROPE), 1.0),
        "page_table": page_table,
        "state_conv": nrm(ks[4], (N_EVEN, DEC_BATCH, CONV_WIDTH - 1, B_WIDTH), 0.5),
        "norm_mix": 1.0 + nrm(ks[6], (DEPTH, D_MODEL), 0.02),
        "norm_ffn": 1.0 + nrm(ks[7], (DEPTH, D_MODEL), 0.02),
        "norm_final": 1.0 + nrm(ks[8], (D_MODEL,), 0.02),
        "w_in_ab": nrm(ks[9], (N_EVEN, D_MODEL, ab_in), D_MODEL ** -0.5),
        "gmlp_ln_g": 1.0 + nrm(ks[10], (N_EVEN, A_WIDTH), 0.02),
        "gmlp_ln_b": nrm(ks[11], (N_EVEN, A_WIDTH), 0.02),
        "w_spatial": nrm(ks[12], (N_EVEN, A_GROUPS, CHUNK, CHUNK), CHUNK ** -0.5),
        "b_spatial": 1.0 + nrm(ks[13], (N_EVEN, A_GROUPS, CHUNK), 0.02),
        "conv_w": nrm(ks[14], (N_EVEN, CONV_WIDTH, B_WIDTH), CONV_WIDTH ** -0.5),
        "conv_b": nrm(ks[15], (N_EVEN, B_WIDTH), 0.02),
        "conv_ln_g": 1.0 + nrm(ks[16], (N_EVEN, B_WIDTH), 0.02),
        "conv_ln_b": nrm(ks[17], (N_EVEN, B_WIDTH), 0.02),
        "w_out_ab": nrm(ks[18], (N_EVEN, A_WIDTH + B_WIDTH, D_MODEL), (A_WIDTH + B_WIDTH) ** -0.5),
        "w_in_c": nrm(ks[19], (N_ODD, D_MODEL, c_in), D_MODEL ** -0.5),
        "q_norm": 1.0 + nrm(ks[20], (N_ODD, Q_LORA), 0.02),
        "kv_norm": 1.0 + nrm(ks[21], (N_ODD, KV_LORA), 0.02),
        "w_uq": nrm(ks[22], (N_ODD, Q_LORA, MLA_HEADS * (QK_NOPE + QK_ROPE)), Q_LORA ** -0.5),
        "w_uk": nrm(ks[23], (N_ODD, KV_LORA, MLA_HEADS, QK_NOPE), KV_LORA ** -0.5),
        "w_uv": nrm(ks[24], (N_ODD, KV_LORA, MLA_HEADS, V_HEAD), KV_LORA ** -0.5),
        "w_out_c": nrm(ks[25], (N_ODD, MLA_HEADS * V_HEAD, D_MODEL), (MLA_HEADS * V_HEAD) ** -0.5),
        "w_gate": nrm(ks[26], (DEPTH, D_MODEL, FFN_HIDDEN), D_MODEL ** -0.5),
        "w_up": nrm(ks[27], (DEPTH, D_MODEL, FFN_HIDDEN), D_MODEL ** -0.5),
        "w_down": nrm(ks[28], (DEPTH, FFN_HIDDEN, D_MODEL), FFN_HIDDEN ** -0.5),
    }


def reference(x_prompt, x_sample, cache_ckv, cache_kpe, page_table, state_conv,
              norm_mix, norm_ffn, norm_final,
              w_in_ab, gmlp_ln_g, gmlp_ln_b, w_spatial, b_spatial, conv_w, conv_b, conv_ln_g, conv_ln_b, w_out_ab,
              w_in_c, q_norm, kv_norm, w_uq, w_uk, w_uv, w_out_c,
              w_gate, w_up, w_down):
    xp, xs = x_prompt, x_sample
    ckv_p, kpe_p, ckv_s, kpe_s = [], [], [], []
    conv_p, conv_s, v_s = [], [], []
    for layer in range(DEPTH):
        hp = rmsnorm(xp, norm_mix[layer])
        hs = rmsnorm(xs, norm_mix[layer])
        if layer % 2 == 0:
            e = layer // 2
            ew = (w_in_ab[e], gmlp_ln_g[e], gmlp_ln_b[e], w_spatial[e], b_spatial[e],
                  conv_w[e], conv_b[e], conv_ln_g[e], conv_ln_b[e], w_out_ab[e])
            zero_prefix = jnp.zeros((xp.shape[0], CONV_WIDTH - 1, B_WIDTH), xp.dtype)
            yp, buf_p, _ = even_mixer(hp, zero_prefix, *ew)
            ys, buf_s, vrows = even_mixer(hs, state_conv[e], *ew)
            conv_p.append(buf_p)
            conv_s.append(buf_s)
            v_s.append(vrows)
        else:
            o = layer // 2
            cw = (w_in_c[o], q_norm[o], kv_norm[o], w_uq[o], w_uk[o], w_uv[o], w_out_c[o])
            yp, ckv_new_p, kpe_new_p = mla_prompt(hp, *cw)
            ys, ckv_new_s, kpe_new_s = mla_sample(hs, cache_ckv, cache_kpe, o, page_table, *cw)
            ckv_p.append(ckv_new_p)
            kpe_p.append(kpe_new_p)
            ckv_s.append(ckv_new_s)
            kpe_s.append(kpe_new_s)
        xp = xp + yp
        xs = xs + ys
        xp = xp + swiglu(rmsnorm(xp, norm_ffn[layer]), w_gate[layer], w_up[layer], w_down[layer])
        xs = xs + swiglu(rmsnorm(xs, norm_ffn[layer]), w_gate[layer], w_up[layer], w_down[layer])
    y_prompt = rmsnorm(xp, norm_final)
    y_sample = rmsnorm(xs, norm_final)
    new_ckv_prompt = jnp.stack(ckv_p)
    new_kpe_prompt = jnp.stack(kpe_p)
    new_ckv_sample = jnp.stack(ckv_s)
    new_kpe_sample = jnp.stack(kpe_s)
    new_conv_prompt = jnp.stack(conv_p)
    new_conv_sample = jnp.stack(conv_s)
    new_gmlp_v_sample = jnp.stack(v_s)
    return (y_prompt, y_sample, new_ckv_prompt, new_kpe_prompt, new_ckv_sample, new_kpe_sample,
            new_conv_prompt, new_conv_sample, new_gmlp_v_sample)
```

```python
import functools
import math

import jax
import jax.numpy as jnp
from jax import lax
from jax.experimental import pallas as pl
from jax.experimental.pallas import tpu as pltpu

F32 = jnp.float32
BF16 = jnp.bfloat16

EPS = 1e-6
CHUNK = 128
A_GROUPS = 4
CONV_WIDTH = 31
CONV_STATE = CONV_WIDTH - 1
MLA_HEADS = 8
QK_NOPE = 128
QK_ROPE = 64
V_HEAD = 128
ROPE_THETA = 10000.0
ATTN_SCALE = (QK_NOPE + QK_ROPE) ** -0.5

LANES = 128
SUBLANES = 8
HALO = 32
QK_PAD = 2 * LANES
VMEM_LIMIT = 56 * 1024 * 1024


def _dot(a, b):
    return jnp.dot(a, b, preferred_element_type=F32)


def _dot_nt(a, b):
    return lax.dot_general(a, b, (((1,), (1,)), ((), ())), preferred_element_type=F32)


def _rms(x, g):
    return x * lax.rsqrt(jnp.mean(x * x, axis=-1, keepdims=True) + EPS) * g


def _ln(x, g, b):
    mu = jnp.mean(x, axis=-1, keepdims=True)
    xc = x - mu
    var = jnp.mean(xc * xc, axis=-1, keepdims=True)
    return xc * lax.rsqrt(var + EPS) * g + b


def _gelu(x):
    return 0.5 * x * (1.0 + jnp.tanh(math.sqrt(2.0 / math.pi) * (x + 0.044715 * (x * x * x))))


def _silu(x):
    return x * jax.nn.sigmoid(x)


def _const_spec(shape):
    nd = len(shape)
    return pl.BlockSpec(shape, lambda *_: (0,) * nd, pipeline_mode=pl.Buffered(1))


def _params(n_axes):
    return pltpu.CompilerParams(dimension_semantics=("arbitrary",) * n_axes,
                                vmem_limit_bytes=VMEM_LIMIT)


def _ffn_body(*refs, hidden, has_proj, has_final, write_x):
    it = iter(refs)
    x_ref = next(it)
    if has_proj:
        o_ref = next(it)
        wo_ref = next(it)
    gn_ref = next(it)
    wgu_ref = next(it)
    wd_ref = next(it)
    if has_final:
        gf_ref = next(it)
    if write_x:
        out_ref = next(it)
    if has_final:
        yf_ref = next(it)

    x = x_ref[...]
    if has_proj:
        x = x + _dot(o_ref[...], wo_ref[...])
    h = _rms(x, gn_ref[...]).astype(BF16)
    g = _dot(h, wgu_ref[:, :hidden])
    u = _dot(h, wgu_ref[:, hidden:])
    a = (_silu(g) * u).astype(BF16)
    xn = x + _dot(a, wd_ref[...])
    if write_x:
        out_ref[...] = xn
    if has_final:
        yf_ref[...] = _rms(xn, gf_ref[...])


def _ffn(x, gn, wgu, wd, *, tm, o=None, wo=None, gf=None, write_x=True):
    rows, d = x.shape
    hidden = wd.shape[0]
    has_proj = o is not None
    has_final = gf is not None
    row_spec = pl.BlockSpec((tm, d), lambda i: (i, 0))
    args, specs = [x], [row_spec]
    if has_proj:
        args += [o, wo]
        specs += [pl.BlockSpec((tm, o.shape[1]), lambda i: (i, 0)), _const_spec(wo.shape)]
    args += [gn, wgu, wd]
    specs += [_const_spec(gn.shape), _const_spec(wgu.shape), _const_spec(wd.shape)]
    if has_final:
        args.append(gf)
        specs.append(_const_spec(gf.shape))
    out_shape, out_specs = [], []
    if write_x:
        out_shape.append(jax.ShapeDtypeStruct((rows, d), F32))
        out_specs.append(row_spec)
    if has_final:
        out_shape.append(jax.ShapeDtypeStruct((rows, d), F32))
        out_specs.append(row_spec)
    res = pl.pallas_call(
        functools.partial(_ffn_body, hidden=hidden, has_proj=has_proj,
                          has_final=has_final, write_x=write_x),
        grid=(rows // tm,),
        in_specs=specs,
        out_specs=out_specs,
        out_shape=out_shape,
        compiler_params=_params(1),
        name="ffn",
    )(*args)
    return res


def _even_prompt_body(x_ref, gn_ref, win_ref, lng_ref, lnb_ref, ws_ref, bs_ref,
                      cw_ref, cb_ref, clg_ref, clb_ref, wout_ref,
                      out_ref, conv_ref, xe_ref, ab_ref, *, tm, tiles_per_seq, aw, bw):
    i = pl.program_id(0)

    @pl.when(i % tiles_per_seq == 0)
    def _():
        xe_ref[0:HALO, :] = jnp.zeros((HALO, bw), F32)

    x = x_ref[...]
    h = _rms(x, gn_ref[...]).astype(BF16)

    u = _gelu(_dot(h, win_ref[:, 0:aw]))
    v = _ln(_gelu(_dot(h, win_ref[:, aw:2 * aw])), lng_ref[...], lnb_ref[...])
    vb = v.astype(BF16)
    gw = aw // A_GROUPS
    for c in range(tm // CHUNK):
        r = slice(c * CHUNK, (c + 1) * CHUNK)
        for g in range(A_GROUPS):
            cs = slice(g * gw, (g + 1) * gw)
            s = _dot(ws_ref[g], vb[r, cs]) + bs_ref[:, cs]
            ab_ref[r, cs] = (u[r, cs] * s).astype(BF16)

    zb = _dot(h, win_ref[:, 2 * aw:2 * aw + bw])
    zg = _dot(h, win_ref[:, 2 * aw + bw:])
    xe_ref[HALO:HALO + tm, :] = zb * jax.nn.sigmoid(zg)

    rb = 2 * SUBLANES
    off = HALO - CONV_STATE
    clg = clg_ref[...]
    clb = clb_ref[...]
    for r0 in range(0, tm, rb):
        acc = jnp.broadcast_to(cb_ref[...], (rb, bw))
        for k in range(CONV_WIDTH):
            acc = acc + cw_ref[k:k + 1, :] * xe_ref[r0 + off + k:r0 + off + k + rb, :]
        ab_ref[r0:r0 + rb, aw:aw + bw] = _silu(_ln(acc, clg, clb)).astype(BF16)

    conv_ref[0] = xe_ref[tm + off:tm + HALO, :]
    xe_ref[0:HALO, :] = xe_ref[tm:tm + HALO, :]
    out_ref[...] = x + _dot(ab_ref[...], wout_ref[...])


def _even_prompt(x, n_seq, gn, win, lng, lnb, ws, bs, cw, cb, clg, clb, wout, *, tm):
    rows, d = x.shape
    aw = lng.shape[1]
    bw = clg.shape[1]
    tiles_per_seq = rows // n_seq // tm
    row_spec = pl.BlockSpec((tm, d), lambda i: (i, 0))
    consts = [gn, win, lng, lnb, ws, bs, cw, cb, clg, clb, wout]
    return pl.pallas_call(
        functools.partial(_even_prompt_body, tm=tm, tiles_per_seq=tiles_per_seq, aw=aw, bw=bw),
        grid=(rows // tm,),
        in_specs=[row_spec] + [_const_spec(c.shape) for c in consts],
        out_specs=[row_spec,
                   pl.BlockSpec((1, CONV_STATE, bw), lambda i: (i // tiles_per_seq, 0, 0))],
        out_shape=[jax.ShapeDtypeStruct((rows, d), F32),
                   jax.ShapeDtypeStruct((n_seq, CONV_STATE, bw), F32)],
        scratch_shapes=[pltpu.VMEM((HALO + tm, bw), F32),
                        pltpu.VMEM((tm, aw + bw), BF16)],
        compiler_params=_params(1),
        name="even_prompt",
    )(x, *consts)


def _even_sample_body(x_ref, st_ref, gn_ref, win_ref, lng_ref, lnb_ref, wsd_ref, bsd_ref,
                      cw_ref, cb_ref, clg_ref, clb_ref, wout_ref,
                      out_ref, v_ref, nst_ref, *, aw, bw):
    x = x_ref[...]
    h = _rms(x, gn_ref[...]).astype(BF16)
    u = _gelu(_dot(h, win_ref[:, 0:aw]))
    v = _ln(_gelu(_dot(h, win_ref[:, aw:2 * aw])), lng_ref[...], lnb_ref[...])
    v_ref[...] = v
    s = v * wsd_ref[...] + bsd_ref[...]
    a_out = u * s

    zb = _dot(h, win_ref[:, 2 * aw:2 * aw + bw])
    zg = _dot(h, win_ref[:, 2 * aw + bw:])
    glu = zb * jax.nn.sigmoid(zg)

    st = st_ref[...]
    c = jnp.sum(st * cw_ref[0:CONV_STATE, :][None], axis=1)
    c = c + glu * cw_ref[CONV_STATE:CONV_WIDTH, :] + cb_ref[...]
    b_out = _silu(_ln(c, clg_ref[...], clb_ref[...]))

    nst_ref[:, 0:CONV_STATE - 1, :] = st[:, 1:CONV_STATE, :]
    nst_ref[:, CONV_STATE - 1:CONV_STATE, :] = glu[:, None, :]

    ab = jnp.concatenate([a_out.astype(BF16), b_out.astype(BF16)], axis=-1)
    out_ref[...] = x + _dot(ab, wout_ref[...])


def _even_sample(x, state, gn, win, lng, lnb, wsd, bsd, cw, cb, clg, clb, wout, *, bs):
    rows, d = x.shape
    aw = lng.shape[1]
    bw = clg.shape[1]
    row_spec = pl.BlockSpec((bs, d), lambda i: (i, 0))
    st_spec = pl.BlockSpec((bs, CONV_STATE, bw), lambda i: (i, 0, 0))
    consts = [gn, win, lng, lnb, wsd, bsd, cw, cb, clg, clb, wout]
    return pl.pallas_call(
        functools.partial(_even_sample_body, aw=aw, bw=bw),
        grid=(rows // bs,),
        in_specs=[row_spec, st_spec] + [_const_spec(c.shape) for c in consts],
        out_specs=[row_spec, pl.BlockSpec((bs, aw), lambda i: (i, 0)), st_spec],
        out_shape=[jax.ShapeDtypeStruct((rows, d), F32),
                   jax.ShapeDtypeStruct((rows, aw), F32),
                   jax.ShapeDtypeStruct(state.shape, F32)],
        compiler_params=_params(1),
        name="even_sample",
    )(x, state, *consts)


def _rope(t, cos, sin_lo, sin_hi):
    half = QK_ROPE // 2
    return (t * cos + pltpu.roll(t, LANES - half, 1) * sin_lo
            + pltpu.roll(t, half, 1) * sin_hi)


def _mla_proj_body(*refs, with_kv, q_lora, kv_lora):
    (x_ref, gn_ref, win_ref, qn_ref, kvn_ref, wuq_ref, wuk_ref, wuv_ref,
     cos_ref, slo_ref, shi_ref) = refs[:11]
    if with_kv:
        q_ref, k_ref, v_ref, ckv_ref, kpe_ref = refs[11:]
    else:
        q_ref, ckv_ref, kpe_ref = refs[11:]

    x = x_ref[...]
    h = _rms(x, gn_ref[...]).astype(BF16)
    z = _dot(h, win_ref[...])
    cq = _rms(z[:, 0:q_lora], qn_ref[...])
    ckv = _rms(z[:, q_lora:q_lora + kv_lora], kvn_ref[...])
    cos, slo, shi = cos_ref[...], slo_ref[...], shi_ref[...]
    kr = _rope(z[:, q_lora + kv_lora:], cos, slo, shi)
    ckv_ref[...] = ckv
    kpe_ref[...] = kr[:, 0:QK_ROPE]

    q = _dot(cq.astype(BF16), wuq_ref[...])
    for hh in range(MLA_HEADS):
        base = hh * QK_PAD
        q_ref[hh, :, 0:QK_NOPE] = (q[:, base:base + QK_NOPE] * ATTN_SCALE).astype(BF16)
        qp = _rope(q[:, base + QK_NOPE:base + QK_PAD], cos, slo, shi)
        q_ref[hh, :, QK_NOPE:QK_PAD] = (qp * ATTN_SCALE).astype(BF16)

    if with_kv:
        cb = ckv.astype(BF16)
        kn = _dot(cb, wuk_ref[...])
        vv = _dot(cb, wuv_ref[...])
        krb = kr.astype(BF16)
        for hh in range(MLA_HEADS):
            k_ref[hh, :, 0:QK_NOPE] = kn[:, hh * QK_NOPE:(hh + 1) * QK_NOPE].astype(BF16)
            k_ref[hh, :, QK_NOPE:QK_PAD] = krb
            v_ref[hh] = vv[:, hh * V_HEAD:(hh + 1) * V_HEAD].astype(BF16)


def _mla_proj(x, gn, win, qn, kvn, wuq, wuk, wuv, cos, slo, shi, *, tm, with_kv):
    rows, d = x.shape
    q_lora = qn.shape[1]
    kv_lora = kvn.shape[1]
    n_pos_tiles = cos.shape[0] // tm
    row_spec = pl.BlockSpec((tm, d), lambda i: (i, 0))
    pos_spec = pl.BlockSpec((tm, LANES), lambda i: (i % n_pos_tiles, 0))
    consts = [gn, win, qn, kvn, wuq, wuk, wuv]
    head_spec = pl.BlockSpec((MLA_HEADS, tm, QK_PAD), lambda i: (0, i, 0))
    out_specs = [head_spec]
    out_shape = [jax.ShapeDtypeStruct((MLA_HEADS, rows, QK_PAD), BF16)]
    if with_kv:
        out_specs += [head_spec, pl.BlockSpec((MLA_HEADS, tm, V_HEAD), lambda i: (0, i, 0))]
        out_shape += [jax.ShapeDtypeStruct((MLA_HEADS, rows, QK_PAD), BF16),
                      jax.ShapeDtypeStruct((MLA_HEADS, rows, V_HEAD), BF16)]
    out_specs += [pl.BlockSpec((tm, kv_lora), lambda i: (i, 0)),
                  pl.BlockSpec((tm, QK_ROPE), lambda i: (i, 0))]
    out_shape += [jax.ShapeDtypeStruct((rows, kv_lora), F32),
                  jax.ShapeDtypeStruct((rows, QK_ROPE), F32)]
    return pl.pallas_call(
        functools.partial(_mla_proj_body, with_kv=with_kv, q_lora=q_lora, kv_lora=kv_lora),
        grid=(rows // tm,),
        in_specs=[row_spec] + [_const_spec(c.shape) for c in consts] + [pos_spec] * 3,
        out_specs=out_specs,
        out_shape=out_shape,
        compiler_params=_params(1),
        name="mla_proj",
    )(x, *consts, cos, slo, shi)


def _flash_body(q_ref, k_ref, v_ref, o_ref, m_ref, l_ref, acc_ref, *, tq):
    qi = pl.program_id(2)
    q = q_ref[0]
    m_ref[...] = jnp.full(m_ref.shape, -jnp.inf, F32)
    l_ref[...] = jnp.zeros(l_ref.shape, F32)
    acc_ref[...] = jnp.zeros(acc_ref.shape, F32)

    def step(start, on_diagonal):
        k = k_ref[0, pl.ds(start, tq), :]
        v = v_ref[0, pl.ds(start, tq), :]
        s = _dot_nt(q, k)
        if on_diagonal:
            row = lax.broadcasted_iota(jnp.int32, (tq, tq), 0)
            col = lax.broadcasted_iota(jnp.int32, (tq, tq), 1)
            s = jnp.where(col <= row, s, -jnp.inf)
        m_prev = m_ref[...]
        m_new = jnp.maximum(m_prev, jnp.max(s, axis=-1, keepdims=True))
        alpha = jnp.exp(m_prev - m_new)
        p = jnp.exp(s - m_new)
        l_ref[...] = alpha * l_ref[...] + jnp.sum(p, axis=-1, keepdims=True)
        acc_ref[...] = alpha * acc_ref[...] + _dot(p.astype(BF16), v)
        m_ref[...] = m_new

    def loop_body(j, carry):
        step(pl.multiple_of(j * tq, tq), False)
        return carry

    lax.fori_loop(0, qi, loop_body, 0)
    step(pl.multiple_of(qi * tq, tq), True)
    o_ref[...] = (acc_ref[...] / l_ref[...]).astype(BF16)


def _flash(q, k, v, n_seq, *, tq):
    heads, rows, _ = q.shape
    seq = rows // n_seq
    nq = seq // tq
    return pl.pallas_call(
        functools.partial(_flash_body, tq=tq),
        grid=(n_seq, heads, nq),
        in_specs=[pl.BlockSpec((1, tq, QK_PAD), lambda b, h, i: (h, b * nq + i, 0)),
                  pl.BlockSpec((1, seq, QK_PAD), lambda b, h, i: (h, b, 0)),
                  pl.BlockSpec((1, seq, V_HEAD), lambda b, h, i: (h, b, 0))],
        out_specs=pl.BlockSpec((tq, V_HEAD), lambda b, h, i: (b * nq + i, h)),
        out_shape=jax.ShapeDtypeStruct((rows, heads * V_HEAD), BF16),
        scratch_shapes=[pltpu.VMEM((tq, 1), F32), pltpu.VMEM((tq, 1), F32),
                        pltpu.VMEM((tq, V_HEAD), F32)],
        compiler_params=_params(3),
        name="flash",
    )(q, k, v)


def _absorb_q_body(q_ref, wukt_ref, o_ref):
    o_ref[0] = _dot(q_ref[0, :, 0:QK_NOPE], wukt_ref[0]).astype(BF16)


def _absorb_q(q, wukt):
    heads, rows, _ = q.shape
    kv_lora = wukt.shape[2]
    return pl.pallas_call(
        _absorb_q_body,
        grid=(heads,),
        in_specs=[pl.BlockSpec((1, rows, QK_PAD), lambda h: (h, 0, 0)),
                  pl.BlockSpec((1, QK_NOPE, kv_lora), lambda h: (h, 0, 0))],
        out_specs=pl.BlockSpec((1, rows, kv_lora), lambda h: (h, 0, 0)),
        out_shape=jax.ShapeDtypeStruct((heads, rows, kv_lora), BF16),
        compiler_params=_params(1),
        name="absorb_q",
    )(q, wukt)


def _page_copy(pt_ref, ckv_hbm, kpe_hbm, ckv_buf, kpe_buf, sem, layer, seq, slot, j, page):
    pid = pt_ref[seq, j]
    rows = pl.ds(j * page, page)
    return (pltpu.make_async_copy(ckv_hbm.at[layer, pid], ckv_buf.at[slot, rows, :], sem.at[0, slot]),
            pltpu.make_async_copy(kpe_hbm.at[layer, pid], kpe_buf.at[slot, rows, :], sem.at[1, slot]))


def _decode_body(pt_ref, ql_ref, qp_ref, cn_ref, kn_ref, ckv_hbm, kpe_hbm, o_ref,
                 ckv_buf, kpe_buf, s_ref, sem, *, layer, n_pages, page, kc):
    b = pl.program_id(0)
    nb = pl.num_programs(0)
    slot = b % 2
    copy = functools.partial(_page_copy, pt_ref, ckv_hbm, kpe_hbm, ckv_buf, kpe_buf, sem, layer)

    def start_all(seq, slt):
        for j in range(n_pages):
            c0, c1 = copy(seq, slt, j, page)
            c0.start()
            c1.start()

    @pl.when(b == 0)
    def _():
        start_all(0, 0)

    @pl.when(b + 1 < nb)
    def _():
        start_all(b + 1, 1 - slot)

    def wait_one(j, carry):
        c0, c1 = copy(b, slot, j, page)
        c0.wait()
        c1.wait()
        return carry

    lax.fori_loop(0, n_pages, wait_one, 0)

    ql = ql_ref[0]
    qp = qp_ref[0]
    past = n_pages * page
    for c in range(past // kc):
        rows = pl.ds(c * kc, kc)
        s_ref[:, c * kc:(c + 1) * kc] = (
            _dot_nt(ql, ckv_buf[slot, rows, :].astype(BF16))
            + _dot_nt(qp, kpe_buf[slot, rows, :].astype(BF16)))

    cn = cn_ref[0]
    kn = kn_ref[0]
    s_new = (jnp.sum(ql.astype(F32) * cn, axis=-1, keepdims=True)
             + jnp.sum(qp.astype(F32) * kn, axis=-1, keepdims=True))
    s = s_ref[...]
    m = jnp.maximum(jnp.max(s, axis=-1, keepdims=True), s_new)
    p = jnp.exp(s - m)
    p_new = jnp.exp(s_new - m)
    denom = jnp.sum(p, axis=-1, keepdims=True) + p_new
    pb = p.astype(BF16)
    acc = p_new * cn
    for c in range(past // kc):
        rows = pl.ds(c * kc, kc)
        acc = acc + _dot(pb[:, c * kc:(c + 1) * kc], ckv_buf[slot, rows, :].astype(BF16))
    o_ref[0] = acc / denom


def _decode_attn(page_table, q_lat, q_pe, ckv_new, kpe_new, cache_ckv, cache_kpe, *, layer):
    n_seq, heads, kv_lora = q_lat.shape
    n_pages = page_table.shape[1]
    page = cache_ckv.shape[2]
    past = n_pages * page
    kc = min(past, 1024)
    grid_spec = pltpu.PrefetchScalarGridSpec(
        num_scalar_prefetch=1,
        grid=(n_seq,),
        in_specs=[pl.BlockSpec((1, heads, kv_lora), lambda b, pt: (b, 0, 0)),
                  pl.BlockSpec((1, heads, QK_ROPE), lambda b, pt: (b, 0, 0)),
                  pl.BlockSpec((1, 1, kv_lora), lambda b, pt: (b, 0, 0)),
                  pl.BlockSpec((1, 1, QK_ROPE), lambda b, pt: (b, 0, 0)),
                  pl.BlockSpec(memory_space=pl.ANY),
                  pl.BlockSpec(memory_space=pl.ANY)],
        out_specs=pl.BlockSpec((1, heads, kv_lora), lambda b, pt: (b, 0, 0)),
        scratch_shapes=[pltpu.VMEM((2, past, kv_lora), F32),
                        pltpu.VMEM((2, past, QK_ROPE), F32),
                        pltpu.VMEM((heads, past), F32),
                        pltpu.SemaphoreType.DMA((2, 2))],
    )
    return pl.pallas_call(
        functools.partial(_decode_body, layer=layer, n_pages=n_pages, page=page, kc=kc),
        grid_spec=grid_spec,
        out_shape=jax.ShapeDtypeStruct((n_seq, heads, kv_lora), F32),
        compiler_params=_params(1),
        name="decode_attn",
    )(page_table, q_lat, q_pe, ckv_new, kpe_new, cache_ckv, cache_kpe)


def _value_up_body(ol_ref, wuv_ref, o_ref):
    o_ref[...] = _dot(ol_ref[0].astype(BF16), wuv_ref[0]).astype(BF16)


def _value_up(o_lat, wuv_h):
    heads, rows, kv_lora = o_lat.shape
    return pl.pallas_call(
        _value_up_body,
        grid=(heads,),
        in_specs=[pl.BlockSpec((1, rows, kv_lora), lambda h: (h, 0, 0)),
                  pl.BlockSpec((1, kv_lora, V_HEAD), lambda h: (h, 0, 0))],
        out_specs=pl.BlockSpec((rows, V_HEAD), lambda h: (0, h)),
        out_shape=jax.ShapeDtypeStruct((rows, heads * V_HEAD), BF16),
        compiler_params=_params(1),
        name="value_up",
    )(o_lat, wuv_h)


def _rope_tables(pos):
    half = QK_ROPE // 2
    inv = ROPE_THETA ** (-jnp.arange(half, dtype=F32) / half)
    ang = pos.astype(F32)[:, None] * inv[None, :]
    c, s = jnp.cos(ang), jnp.sin(ang)
    z = jnp.zeros_like(c)
    zz = jnp.zeros((pos.shape[0], LANES - QK_ROPE), F32)
    cos = jnp.concatenate([c, c, zz], axis=1)
    sin_lo = jnp.concatenate([-s, z, zz], axis=1)
    sin_hi = jnp.concatenate([z, s, zz], axis=1)
    return cos, sin_lo, sin_hi


def _row(v):
    return v.reshape(1, -1)


def kernel(x_prompt, x_sample, cache_ckv, cache_kpe, page_table, state_conv, norm_mix, norm_ffn, norm_final, w_in_ab, gmlp_ln_g, gmlp_ln_b, w_spatial, b_spatial, conv_w, conv_b, conv_ln_g, conv_ln_b, w_out_ab, w_in_c, q_norm, kv_norm, w_uq, w_uk, w_uv, w_out_c, w_gate, w_up, w_down):
    n_seq, seq, d = x_prompt.shape
    n_dec, dec_seq, _ = x_sample.shape
    assert dec_seq == 1, "sample group handles one new token per sequence"
    depth = norm_mix.shape[0]
    aw = gmlp_ln_g.shape[1]
    q_lora = q_norm.shape[1]
    kv_lora = kv_norm.shape[1]
    past = page_table.shape[1] * cache_ckv.shape[2]
    assert past % CHUNK == 0 and seq % CHUNK == 0

    tm = min(256, seq)
    tq = min(512, seq)
    bs = min(32, n_dec)

    xp = x_prompt.reshape(n_seq * seq, d)
    xs = x_sample.reshape(n_dec, d)

    tril = jnp.tril(jnp.ones((CHUNK, CHUNK), dtype=bool))
    cos_p, slo_p, shi_p = _rope_tables(jnp.arange(seq))
    cos_s, slo_s, shi_s = _rope_tables(jnp.full((n_dec,), past))

    ckv_p, kpe_p, ckv_s, kpe_s, conv_p, conv_s, v_s = [], [], [], [], [], [], []
    for layer in range(depth):
        gn = _row(norm_mix[layer])
        gf = _row(norm_ffn[layer])
        wgu = jnp.concatenate([w_gate[layer], w_up[layer]], axis=1).astype(BF16)
        wd = w_down[layer].astype(BF16)
        last = layer == depth - 1
        nf = _row(norm_final) if last else None
        if layer % 2 == 0:
            e = layer // 2
            win = w_in_ab[e].astype(BF16)
            wout = w_out_ab[e].astype(BF16)
            ws = jnp.where(tril[None], w_spatial[e], 0.0).astype(BF16)
            gw = aw // A_GROUPS
            bs_full = jnp.repeat(b_spatial[e].T, gw, axis=1)
            wsd = _row(jnp.repeat(w_spatial[e][:, 0, 0], gw))
            bsd = _row(jnp.repeat(b_spatial[e][:, 0], gw))
            shared = (_row(gmlp_ln_g[e]), _row(gmlp_ln_b[e]))
            convp = (conv_w[e], _row(conv_b[e]), _row(conv_ln_g[e]), _row(conv_ln_b[e]), wout)
            xp, buf_p = _even_prompt(xp, n_seq, gn, win, *shared, ws, bs_full, *convp, tm=tm)
            xs, vrows, buf_s = _even_sample(xs, state_conv[e], gn, win, *shared, wsd, bsd,
                                            *convp, bs=bs)
            conv_p.append(buf_p)
            conv_s.append(buf_s)
            v_s.append(vrows.reshape(n_dec, 1, aw))
            ffn_p = dict()
            ffn_s = dict()
        else:
            o = layer // 2
            pad = jnp.zeros((d, LANES - QK_ROPE), F32)
            win = jnp.concatenate([w_in_c[o], pad], axis=1).astype(BF16)
            wq = w_uq[o].reshape(q_lora, MLA_HEADS, QK_NOPE + QK_ROPE)
            wq = jnp.pad(wq, ((0, 0), (0, 0), (0, QK_PAD - QK_NOPE - QK_ROPE)))
            wq = wq.reshape(q_lora, MLA_HEADS * QK_PAD).astype(BF16)
            wuk = w_uk[o].reshape(kv_lora, MLA_HEADS * QK_NOPE).astype(BF16)
            wuv = w_uv[o].reshape(kv_lora, MLA_HEADS * V_HEAD).astype(BF16)
            wukt = jnp.transpose(w_uk[o], (1, 2, 0)).astype(BF16)
            wuv_h = jnp.transpose(w_uv[o], (1, 0, 2)).astype(BF16)
            woc = w_out_c[o].astype(BF16)
            pw = (gn, win, _row(q_norm[o]), _row(kv_norm[o]), wq, wuk, wuv)

            q, k, v, ckv, kpe = _mla_proj(xp, *pw, cos_p, slo_p, shi_p, tm=tm, with_kv=True)
            attn_p = _flash(q, k, v, n_seq, tq=tq)
            ckv_p.append(ckv.reshape(n_seq, seq, kv_lora))
            kpe_p.append(kpe.reshape(n_seq, seq, QK_ROPE))

            qs, ckv_n, kpe_n = _mla_proj(xs, *pw, cos_s, slo_s, shi_s, tm=n_dec, with_kv=False)
            q_lat = jnp.transpose(_absorb_q(qs, wukt), (1, 0, 2))
            q_pe = jnp.transpose(qs[:, :, QK_NOPE:QK_NOPE + QK_ROPE], (1, 0, 2))
            o_lat = _decode_attn(page_table, q_lat, q_pe,
                                 ckv_n.reshape(n_dec, 1, kv_lora), kpe_n.reshape(n_dec, 1, QK_ROPE),
                                 cache_ckv, cache_kpe, layer=o)
            attn_s = _value_up(jnp.transpose(o_lat, (1, 0, 2)), wuv_h)
            ckv_s.append(ckv_n.reshape(n_dec, 1, kv_lora))
            kpe_s.append(kpe_n.reshape(n_dec, 1, QK_ROPE))
            ffn_p = dict(o=attn_p, wo=woc)
            ffn_s = dict(o=attn_s, wo=woc)

        res_p = _ffn(xp, gf, wgu, wd, tm=tm, gf=nf, write_x=not last, **ffn_p)
        res_s = _ffn(xs, gf, wgu, wd, tm=n_dec, gf=nf, write_x=not last, **ffn_s)
        xp, xs = res_p[0], res_s[0]

    y_prompt = xp.reshape(n_seq, seq, d)
    y_sample = xs.reshape(n_dec, 1, d)
    return (y_prompt, y_sample, jnp.stack(ckv_p), jnp.stack(kpe_p), jnp.stack(ckv_s),
            jnp.stack(kpe_s), jnp.stack(conv_p), jnp.stack(conv_s), jnp.stack(v_s))
```

```python
import functools
import math

import jax
import jax.numpy as jnp
from jax import lax
from jax.experimental import pallas as pl
from jax.experimental.pallas import tpu as pltpu

F32 = jnp.float32
BF16 = jnp.bfloat16

EPS = 1e-6
CHUNK = 128
A_GROUPS = 4
CONV_WIDTH = 31
CONV_STATE = CONV_WIDTH - 1
MLA_HEADS = 8
QK_NOPE = 128
QK_ROPE = 64
V_HEAD = 128
ROPE_THETA = 10000.0
ATTN_SCALE = (QK_NOPE + QK_ROPE) ** -0.5
Q_SCALE = ATTN_SCALE * math.log2(math.e)

LANES = 128
SUBLANES = 8
HALO = 32
CONV_ROW_BLOCK = 64
QK_PAD = 2 * LANES
VMEM_LIMIT = 56 * 1024 * 1024


def _dot(a, b):
    return jnp.dot(a, b, preferred_element_type=F32)


def _dot_nt(a, b):
    return lax.dot_general(a, b, (((1,), (1,)), ((), ())), preferred_element_type=F32)


def _rms(x, g):
    return x * lax.rsqrt(jnp.mean(x * x, axis=-1, keepdims=True) + EPS) * g


def _ln(x, g, b):
    mu = jnp.mean(x, axis=-1, keepdims=True)
    xc = x - mu
    var = jnp.mean(xc * xc, axis=-1, keepdims=True)
    return xc * lax.rsqrt(var + EPS) * g + b


def _gelu(x):
    return 0.5 * x * (1.0 + jnp.tanh(math.sqrt(2.0 / math.pi) * (x + 0.044715 * (x * x * x))))


def _silu(x):
    return x * jax.nn.sigmoid(x)


def _const_spec(shape):
    nd = len(shape)
    return pl.BlockSpec(shape, lambda *_: (0,) * nd, pipeline_mode=pl.Buffered(1))


def _params(n_axes):
    return pltpu.CompilerParams(dimension_semantics=("arbitrary",) * n_axes,
                                vmem_limit_bytes=VMEM_LIMIT)


def _ffn_body(*refs, hidden, has_proj, has_final, write_x):
    it = iter(refs)
    x_ref = next(it)
    if has_proj:
        o_ref = next(it)
        wo_ref = next(it)
    gn_ref = next(it)
    wgu_ref = next(it)
    wd_ref = next(it)
    if has_final:
        gf_ref = next(it)
    if write_x:
        out_ref = next(it)
    if has_final:
        yf_ref = next(it)

    x = x_ref[...]
    if has_proj:
        x = x + _dot(o_ref[...], wo_ref[...])
    h = _rms(x, gn_ref[...]).astype(BF16)
    g = _dot(h, wgu_ref[:, :hidden])
    u = _dot(h, wgu_ref[:, hidden:])
    a = (_silu(g) * u).astype(BF16)
    xn = x + _dot(a, wd_ref[...])
    if write_x:
        out_ref[...] = xn
    if has_final:
        yf_ref[...] = _rms(xn, gf_ref[...])


def _ffn(x, gn, wgu, wd, *, tm, o=None, wo=None, gf=None, write_x=True):
    rows, d = x.shape
    hidden = wd.shape[0]
    has_proj = o is not None
    has_final = gf is not None
    row_spec = pl.BlockSpec((tm, d), lambda i: (i, 0))
    args, specs = [x], [row_spec]
    if has_proj:
        args += [o, wo]
        specs += [pl.BlockSpec((tm, o.shape[1]), lambda i: (i, 0)), _const_spec(wo.shape)]
    args += [gn, wgu, wd]
    specs += [_const_spec(gn.shape), _const_spec(wgu.shape), _const_spec(wd.shape)]
    if has_final:
        args.append(gf)
        specs.append(_const_spec(gf.shape))
    out_shape, out_specs = [], []
    if write_x:
        out_shape.append(jax.ShapeDtypeStruct((rows, d), F32))
        out_specs.append(row_spec)
    if has_final:
        out_shape.append(jax.ShapeDtypeStruct((rows, d), F32))
        out_specs.append(row_spec)
    res = pl.pallas_call(
        functools.partial(_ffn_body, hidden=hidden, has_proj=has_proj,
                          has_final=has_final, write_x=write_x),
        grid=(rows // tm,),
        in_specs=specs,
        out_specs=out_specs,
        out_shape=out_shape,
        compiler_params=_params(1),
        name="ffn",
    )(*args)
    return res


def _even_prompt_body(x_ref, gn_ref, win_ref, lng_ref, lnb_ref, ws_ref, bs_ref,
                      cw_ref, cb_ref, clg_ref, clb_ref, wout_ref,
                      out_ref, conv_ref, xe_ref, cv_ref, ab_ref, *, tm, tiles_per_seq, aw, bw):
    i = pl.program_id(0)

    @pl.when(i % tiles_per_seq == 0)
    def _():
        xe_ref[0:HALO, :] = jnp.zeros((HALO, bw), F32)

    x = x_ref[...]
    h = _rms(x, gn_ref[...]).astype(BF16)

    u = _gelu(_dot(h, win_ref[:, 0:aw]))
    v = _ln(_gelu(_dot(h, win_ref[:, aw:2 * aw])), lng_ref[...], lnb_ref[...])
    vb = v.astype(BF16)
    gw = aw // A_GROUPS
    for c in range(tm // CHUNK):
        r = slice(c * CHUNK, (c + 1) * CHUNK)
        for g in range(A_GROUPS):
            cs = slice(g * gw, (g + 1) * gw)
            s = _dot(ws_ref[g], vb[r, cs]) + bs_ref[:, cs]
            ab_ref[r, cs] = (u[r, cs] * s).astype(BF16)

    zb = _dot(h, win_ref[:, 2 * aw:2 * aw + bw])
    zg = _dot(h, win_ref[:, 2 * aw + bw:])
    xe_ref[HALO:HALO + tm, :] = zb * jax.nn.sigmoid(zg)

    off = HALO - CONV_STATE
    rb = min(tm, CONV_ROW_BLOCK)
    for r0 in range(0, tm, rb):
        for c0 in range(0, bw, LANES):
            cs = slice(c0, c0 + LANES)
            acc = jnp.broadcast_to(cb_ref[:, cs], (rb, LANES))
            for r in range(SUBLANES):
                taps = [a for a in range(HALO // SUBLANES + 1)
                        if 0 <= a * SUBLANES + r - off < CONV_WIDTH]
                lo = r0 + taps[0] * SUBLANES
                span = rb + (taps[-1] - taps[0]) * SUBLANES
                if r == 0:
                    win = xe_ref[lo:lo + span, cs]
                else:
                    win = pltpu.roll(xe_ref[lo:lo + span + SUBLANES, cs], span + SUBLANES - r, 0)
                for a in taps:
                    k = a * SUBLANES + r - off
                    sh = (a - taps[0]) * SUBLANES
                    acc = acc + cw_ref[k:k + 1, cs] * win[sh:sh + rb]
            cv_ref[r0:r0 + rb, cs] = acc
    ab_ref[:, aw:aw + bw] = _silu(_ln(cv_ref[...], clg_ref[...], clb_ref[...])).astype(BF16)

    conv_ref[0] = xe_ref[tm + off:tm + HALO, :]
    xe_ref[0:HALO, :] = xe_ref[tm:tm + HALO, :]
    out_ref[...] = x + _dot(ab_ref[...], wout_ref[...])


def _even_prompt(x, n_seq, gn, win, lng, lnb, ws, bs, cw, cb, clg, clb, wout, *, tm):
    rows, d = x.shape
    aw = lng.shape[1]
    bw = clg.shape[1]
    tiles_per_seq = rows // n_seq // tm
    row_spec = pl.BlockSpec((tm, d), lambda i: (i, 0))
    consts = [gn, win, lng, lnb, ws, bs, cw, cb, clg, clb, wout]
    return pl.pallas_call(
        functools.partial(_even_prompt_body, tm=tm, tiles_per_seq=tiles_per_seq, aw=aw, bw=bw),
        grid=(rows // tm,),
        in_specs=[row_spec] + [_const_spec(c.shape) for c in consts],
        out_specs=[row_spec,
                   pl.BlockSpec((1, CONV_STATE, bw), lambda i: (i // tiles_per_seq, 0, 0))],
        out_shape=[jax.ShapeDtypeStruct((rows, d), F32),
                   jax.ShapeDtypeStruct((n_seq, CONV_STATE, bw), F32)],
        scratch_shapes=[pltpu.VMEM((HALO + tm, bw), F32),
                        pltpu.VMEM((tm, bw), F32),
                        pltpu.VMEM((tm, aw + bw), BF16)],
        compiler_params=_params(1),
        name="even_prompt",
    )(x, *consts)


def _even_sample_body(x_ref, st_ref, gn_ref, win_ref, lng_ref, lnb_ref, wsd_ref, bsd_ref,
                      cw_ref, cb_ref, clg_ref, clb_ref, wout_ref,
                      out_ref, v_ref, nst_ref, *, aw, bw):
    x = x_ref[...]
    h = _rms(x, gn_ref[...]).astype(BF16)
    u = _gelu(_dot(h, win_ref[:, 0:aw]))
    v = _ln(_gelu(_dot(h, win_ref[:, aw:2 * aw])), lng_ref[...], lnb_ref[...])
    v_ref[...] = v
    s = v * wsd_ref[...] + bsd_ref[...]
    a_out = u * s

    zb = _dot(h, win_ref[:, 2 * aw:2 * aw + bw])
    zg = _dot(h, win_ref[:, 2 * aw + bw:])
    glu = zb * jax.nn.sigmoid(zg)

    st = st_ref[...]
    c = jnp.sum(st * cw_ref[0:CONV_STATE, :][None], axis=1)
    c = c + glu * cw_ref[CONV_STATE:CONV_WIDTH, :] + cb_ref[...]
    b_out = _silu(_ln(c, clg_ref[...], clb_ref[...]))

    nst_ref[:, 0:CONV_STATE - 1, :] = st[:, 1:CONV_STATE, :]
    nst_ref[:, CONV_STATE - 1:CONV_STATE, :] = glu[:, None, :]

    ab = jnp.concatenate([a_out.astype(BF16), b_out.astype(BF16)], axis=-1)
    out_ref[...] = x + _dot(ab, wout_ref[...])


def _even_sample(x, state, gn, win, lng, lnb, wsd, bsd, cw, cb, clg, clb, wout, *, bs):
    rows, d = x.shape
    aw = lng.shape[1]
    bw = clg.shape[1]
    row_spec = pl.BlockSpec((bs, d), lambda i: (i, 0))
    st_spec = pl.BlockSpec((bs, CONV_STATE, bw), lambda i: (i, 0, 0))
    consts = [gn, win, lng, lnb, wsd, bsd, cw, cb, clg, clb, wout]
    return pl.pallas_call(
        functools.partial(_even_sample_body, aw=aw, bw=bw),
        grid=(rows // bs,),
        in_specs=[row_spec, st_spec] + [_const_spec(c.shape) for c in consts],
        out_specs=[row_spec, pl.BlockSpec((bs, aw), lambda i: (i, 0)), st_spec],
        out_shape=[jax.ShapeDtypeStruct((rows, d), F32),
                   jax.ShapeDtypeStruct((rows, aw), F32),
                   jax.ShapeDtypeStruct(state.shape, F32)],
        compiler_params=_params(1),
        name="even_sample",
    )(x, state, *consts)


def _rope(t, cos, sin_lo, sin_hi):
    half = QK_ROPE // 2
    return (t * cos + pltpu.roll(t, LANES - half, 1) * sin_lo
            + pltpu.roll(t, half, 1) * sin_hi)


def _mla_proj_body(*refs, with_kv, q_lora, kv_lora):
    (x_ref, gn_ref, win_ref, qn_ref, kvn_ref, wuq_ref, wuk_ref, wuvt_ref,
     cos_ref, slo_ref, shi_ref) = refs[:11]
    if with_kv:
        q_ref, k_ref, vt_ref, ckv_ref, kpe_ref = refs[11:]
    else:
        q_ref, ckv_ref, kpe_ref = refs[11:]

    x = x_ref[...]
    h = _rms(x, gn_ref[...]).astype(BF16)
    z = _dot(h, win_ref[...])
    cq = _rms(z[:, 0:q_lora], qn_ref[...])
    ckv = _rms(z[:, q_lora:q_lora + kv_lora], kvn_ref[...])
    cos, slo, shi = cos_ref[...], slo_ref[...], shi_ref[...]
    kr = _rope(z[:, q_lora + kv_lora:], cos, slo, shi)
    ckv_ref[...] = ckv
    kpe_ref[...] = kr[:, 0:QK_ROPE]

    q = _dot(cq.astype(BF16), wuq_ref[...])
    for hh in range(MLA_HEADS):
        base = hh * QK_PAD
        q_ref[hh, :, 0:QK_NOPE] = (q[:, base:base + QK_NOPE] * Q_SCALE).astype(BF16)
        qp = _rope(q[:, base + QK_NOPE:base + QK_PAD], cos, slo, shi)
        q_ref[hh, :, QK_NOPE:QK_PAD] = (qp * Q_SCALE).astype(BF16)

    if with_kv:
        cb = ckv.astype(BF16)
        kn = _dot(cb, wuk_ref[...])
        vt = _dot_nt(wuvt_ref[...], cb)
        krb = kr.astype(BF16)
        for hh in range(MLA_HEADS):
            k_ref[hh, :, 0:QK_NOPE] = kn[:, hh * QK_NOPE:(hh + 1) * QK_NOPE].astype(BF16)
            k_ref[hh, :, QK_NOPE:QK_PAD] = krb
            vt_ref[hh, 0] = vt[hh * V_HEAD:(hh + 1) * V_HEAD, :].astype(BF16)


def _mla_proj(x, gn, win, qn, kvn, wuq, wuk, wuvt, cos, slo, shi, *, tm, with_kv, tk=None):
    rows, d = x.shape
    q_lora = qn.shape[1]
    kv_lora = kvn.shape[1]
    n_pos_tiles = cos.shape[0] // tm
    row_spec = pl.BlockSpec((tm, d), lambda i: (i, 0))
    pos_spec = pl.BlockSpec((tm, LANES), lambda i: (i % n_pos_tiles, 0))
    consts = [gn, win, qn, kvn, wuq, wuk, wuvt]
    head_spec = pl.BlockSpec((MLA_HEADS, tm, QK_PAD), lambda i: (0, i, 0))
    out_specs = [head_spec]
    out_shape = [jax.ShapeDtypeStruct((MLA_HEADS, rows, QK_PAD), BF16)]
    if with_kv:
        per_tile = tk // tm
        out_specs += [head_spec,
                      pl.BlockSpec((MLA_HEADS, 1, V_HEAD, tm),
                                   lambda i: (0, i // per_tile, 0, i % per_tile))]
        out_shape += [jax.ShapeDtypeStruct((MLA_HEADS, rows, QK_PAD), BF16),
                      jax.ShapeDtypeStruct((MLA_HEADS, rows // tk, V_HEAD, tk), BF16)]
    out_specs += [pl.BlockSpec((tm, kv_lora), lambda i: (i, 0)),
                  pl.BlockSpec((tm, QK_ROPE), lambda i: (i, 0))]
    out_shape += [jax.ShapeDtypeStruct((rows, kv_lora), F32),
                  jax.ShapeDtypeStruct((rows, QK_ROPE), F32)]
    return pl.pallas_call(
        functools.partial(_mla_proj_body, with_kv=with_kv, q_lora=q_lora, kv_lora=kv_lora),
        grid=(rows // tm,),
        in_specs=[row_spec] + [_const_spec(c.shape) for c in consts] + [pos_spec] * 3,
        out_specs=out_specs,
        out_shape=out_shape,
        compiler_params=_params(1),
        name="mla_proj",
    )(x, *consts, cos, slo, shi)


def _flash_body(q_ref, k_ref, vt_ref, o_ref, st_ref, mt_ref, m_ref, l_ref, acc_ref, *, tq):
    qi = pl.program_id(2)
    q = q_ref[0]
    m_ref[...] = jnp.full(m_ref.shape, -jnp.inf, F32)
    l_ref[...] = jnp.zeros(l_ref.shape, F32)
    acc_ref[...] = jnp.zeros(acc_ref.shape, F32)

    def scores(j):
        return _dot_nt(k_ref[0, pl.ds(pl.multiple_of(j * tq, tq), tq), :], q)

    def consume(j, st, mt):
        m_prev = m_ref[...]
        m_new = jnp.maximum(m_prev, mt)
        alpha = jnp.exp2(m_prev - m_new)
        pt = jnp.exp2(st - m_new)
        l_ref[...] = alpha * l_ref[...] + jnp.sum(pt, axis=0, keepdims=True)
        acc_ref[...] = alpha * acc_ref[...] + _dot(vt_ref[0, j], pt.astype(BF16))
        m_ref[...] = m_new

    st0 = scores(0)
    st_ref[...] = st0
    mt_ref[...] = jnp.max(st0, axis=0, keepdims=True)

    def loop_body(j, carry):
        st = st_ref[...]
        mt = mt_ref[...]
        st_next = scores(j + 1)
        consume(j, st, mt)
        st_ref[...] = st_next
        mt_ref[...] = jnp.max(st_next, axis=0, keepdims=True)
        return carry

    lax.fori_loop(0, qi, loop_body, 0)
    key = lax.broadcasted_iota(jnp.int32, (tq, tq), 0)
    qry = lax.broadcasted_iota(jnp.int32, (tq, tq), 1)
    st = jnp.where(key <= qry, st_ref[...], -jnp.inf)
    consume(qi, st, jnp.max(st, axis=0, keepdims=True))
    o_ref[...] = (acc_ref[...] / l_ref[...]).T.astype(BF16)


def _flash(q, k, vt, n_seq, *, tq):
    heads, rows, _ = q.shape
    seq = rows // n_seq
    nq = seq // tq
    return pl.pallas_call(
        functools.partial(_flash_body, tq=tq),
        grid=(n_seq, heads, nq),
        in_specs=[pl.BlockSpec((1, tq, QK_PAD), lambda b, h, i: (h, b * nq + i, 0)),
                  pl.BlockSpec((1, seq, QK_PAD), lambda b, h, i: (h, b, 0)),
                  pl.BlockSpec((1, nq, V_HEAD, tq), lambda b, h, i: (h, b, 0, 0))],
        out_specs=pl.BlockSpec((tq, V_HEAD), lambda b, h, i: (b * nq + i, h)),
        out_shape=jax.ShapeDtypeStruct((rows, heads * V_HEAD), BF16),
        scratch_shapes=[pltpu.VMEM((tq, tq), F32), pltpu.VMEM((1, tq), F32),
                        pltpu.VMEM((1, tq), F32), pltpu.VMEM((1, tq), F32),
                        pltpu.VMEM((V_HEAD, tq), F32)],
        compiler_params=_params(3),
        name="flash",
    )(q, k, vt)


def _absorb_q_body(q_ref, wukt_ref, o_ref):
    o_ref[0] = _dot(q_ref[0, :, 0:QK_NOPE], wukt_ref[0]).astype(BF16)


def _absorb_q(q, wukt):
    heads, rows, _ = q.shape
    kv_lora = wukt.shape[2]
    return pl.pallas_call(
        _absorb_q_body,
        grid=(heads,),
        in_specs=[pl.BlockSpec((1, rows, QK_PAD), lambda h: (h, 0, 0)),
                  pl.BlockSpec((1, QK_NOPE, kv_lora), lambda h: (h, 0, 0))],
        out_specs=pl.BlockSpec((1, rows, kv_lora), lambda h: (h, 0, 0)),
        out_shape=jax.ShapeDtypeStruct((heads, rows, kv_lora), BF16),
        compiler_params=_params(1),
        name="absorb_q",
    )(q, wukt)


def _page_copy(pt_ref, ckv_hbm, kpe_hbm, ckv_buf, kpe_buf, sem, layer, seq, slot, j, page):
    pid = pt_ref[seq, j]
    keys = pl.ds(j * page, page)
    return (pltpu.make_async_copy(ckv_hbm.at[layer, pid], ckv_buf.at[slot, keys, :], sem.at[0, slot]),
            pltpu.make_async_copy(kpe_hbm.at[layer, pid], kpe_buf.at[slot, :, keys], sem.at[1, slot]))


def _decode_body(pt_ref, ql_ref, qp_ref, cn_ref, kn_ref, ckv_hbm, kpe_hbm, o_ref,
                 ckv_buf, kpe_buf, s_ref, sem, *, layer, n_pages, page, kc):
    b = pl.program_id(0)
    nb = pl.num_programs(0)
    slot = b % 2
    copy = functools.partial(_page_copy, pt_ref, ckv_hbm, kpe_hbm, ckv_buf, kpe_buf, sem, layer)

    def start_all(seq, slt):
        for j in range(n_pages):
            c0, c1 = copy(seq, slt, j, page)
            c0.start()
            c1.start()

    @pl.when(b == 0)
    def _():
        start_all(0, 0)

    @pl.when(b + 1 < nb)
    def _():
        start_all(b + 1, 1 - slot)

    def wait_one(j, carry):
        c0, c1 = copy(b, slot, j, page)
        c0.wait()
        c1.wait()
        return carry

    lax.fori_loop(0, n_pages, wait_one, 0)

    ql = ql_ref[0]
    qp = qp_ref[0]
    past = n_pages * page
    for c in range(past // kc):
        keys = pl.ds(c * kc, kc)
        s_ref[:, c * kc:(c + 1) * kc] = (
            _dot_nt(ql, ckv_buf[slot, keys, :].astype(BF16))
            + _dot(qp, kpe_buf[slot, :, keys].astype(BF16)))

    cn = cn_ref[0]
    kn = kn_ref[0]
    s_new = (jnp.sum(ql.astype(F32) * cn, axis=-1, keepdims=True)
             + jnp.sum(qp.astype(F32) * kn, axis=-1, keepdims=True))
    s = s_ref[...]
    m = jnp.maximum(jnp.max(s, axis=-1, keepdims=True), s_new)
    p = jnp.exp2(s - m)
    p_new = jnp.exp2(s_new - m)
    denom = jnp.sum(p, axis=-1, keepdims=True) + p_new
    pb = p.astype(BF16)
    acc = p_new * cn
    for c in range(past // kc):
        keys = pl.ds(c * kc, kc)
        acc = acc + _dot(pb[:, c * kc:(c + 1) * kc], ckv_buf[slot, keys, :].astype(BF16))
    o_ref[0] = acc / denom


def _decode_attn(page_table, q_lat, q_pe, ckv_new, kpe_new, cache_ckv, cache_kpe_t, *, layer):
    n_seq, heads, kv_lora = q_lat.shape
    n_pages = page_table.shape[1]
    page = cache_ckv.shape[2]
    past = n_pages * page
    kc = min(past, 1024)
    grid_spec = pltpu.PrefetchScalarGridSpec(
        num_scalar_prefetch=1,
        grid=(n_seq,),
        in_specs=[pl.BlockSpec((1, heads, kv_lora), lambda b, pt: (b, 0, 0)),
                  pl.BlockSpec((1, heads, QK_ROPE), lambda b, pt: (b, 0, 0)),
                  pl.BlockSpec((1, 1, kv_lora), lambda b, pt: (b, 0, 0)),
                  pl.BlockSpec((1, 1, QK_ROPE), lambda b, pt: (b, 0, 0)),
                  pl.BlockSpec(memory_space=pl.ANY),
                  pl.BlockSpec(memory_space=pl.ANY)],
        out_specs=pl.BlockSpec((1, heads, kv_lora), lambda b, pt: (b, 0, 0)),
        scratch_shapes=[pltpu.VMEM((2, past, kv_lora), F32),
                        pltpu.VMEM((2, QK_ROPE, past), F32),
                        pltpu.VMEM((heads, past), F32),
                        pltpu.SemaphoreType.DMA((2, 2))],
    )
    return pl.pallas_call(
        functools.partial(_decode_body, layer=layer, n_pages=n_pages, page=page, kc=kc),
        grid_spec=grid_spec,
        out_shape=jax.ShapeDtypeStruct((n_seq, heads, kv_lora), F32),
        compiler_params=_params(1),
        name="decode_attn",
    )(page_table, q_lat, q_pe, ckv_new, kpe_new, cache_ckv, cache_kpe_t)


def _value_up_body(ol_ref, wuv_ref, o_ref):
    o_ref[...] = _dot(ol_ref[0].astype(BF16), wuv_ref[0]).astype(BF16)


def _value_up(o_lat, wuv_h):
    heads, rows, kv_lora = o_lat.shape
    return pl.pallas_call(
        _value_up_body,
        grid=(heads,),
        in_specs=[pl.BlockSpec((1, rows, kv_lora), lambda h: (h, 0, 0)),
                  pl.BlockSpec((1, kv_lora, V_HEAD), lambda h: (h, 0, 0))],
        out_specs=pl.BlockSpec((rows, V_HEAD), lambda h: (0, h)),
        out_shape=jax.ShapeDtypeStruct((rows, heads * V_HEAD), BF16),
        compiler_params=_params(1),
        name="value_up",
    )(o_lat, wuv_h)


def _rope_tables(pos):
    half = QK_ROPE // 2
    inv = ROPE_THETA ** (-jnp.arange(half, dtype=F32) / half)
    ang = pos.astype(F32)[:, None] * inv[None, :]
    c, s = jnp.cos(ang), jnp.sin(ang)
    z = jnp.zeros_like(c)
    zz = jnp.zeros((pos.shape[0], LANES - QK_ROPE), F32)
    cos = jnp.concatenate([c, c, zz], axis=1)
    sin_lo = jnp.concatenate([-s, z, zz], axis=1)
    sin_hi = jnp.concatenate([z, s, zz], axis=1)
    return cos, sin_lo, sin_hi


def _row(v):
    return v.reshape(1, -1)


def kernel(x_prompt, x_sample, cache_ckv, cache_kpe, page_table, state_conv, norm_mix, norm_ffn, norm_final, w_in_ab, gmlp_ln_g, gmlp_ln_b, w_spatial, b_spatial, conv_w, conv_b, conv_ln_g, conv_ln_b, w_out_ab, w_in_c, q_norm, kv_norm, w_uq, w_uk, w_uv, w_out_c, w_gate, w_up, w_down):
    n_seq, seq, d = x_prompt.shape
    n_dec, dec_seq, _ = x_sample.shape
    assert dec_seq == 1, "sample group handles one new token per sequence"
    depth = norm_mix.shape[0]
    aw = gmlp_ln_g.shape[1]
    q_lora = q_norm.shape[1]
    kv_lora = kv_norm.shape[1]
    past = page_table.shape[1] * cache_ckv.shape[2]
    assert past % CHUNK == 0 and seq % CHUNK == 0

    tm = min(256, seq)
    tp = min(512, seq)
    tq = min(1024, seq)
    bs = min(32, n_dec)

    xp = x_prompt.reshape(n_seq * seq, d)
    xs = x_sample.reshape(n_dec, d)
    cache_kpe_t = jnp.swapaxes(cache_kpe, 2, 3)

    tril = jnp.tril(jnp.ones((CHUNK, CHUNK), dtype=bool))
    cos_p, slo_p, shi_p = _rope_tables(jnp.arange(seq))
    cos_s, slo_s, shi_s = _rope_tables(jnp.full((n_dec,), past))

    ckv_p, kpe_p, ckv_s, kpe_s, conv_p, conv_s, v_s = [], [], [], [], [], [], []
    for layer in range(depth):
        gn = _row(norm_mix[layer])
        gf = _row(norm_ffn[layer])
        wgu = jnp.concatenate([w_gate[layer], w_up[layer]], axis=1).astype(BF16)
        wd = w_down[layer].astype(BF16)
        last = layer == depth - 1
        nf = _row(norm_final) if last else None
        if layer % 2 == 0:
            e = layer // 2
            win = w_in_ab[e].astype(BF16)
            wout = w_out_ab[e].astype(BF16)
            ws = jnp.where(tril[None], w_spatial[e], 0.0).astype(BF16)
            gw = aw // A_GROUPS
            bs_full = jnp.repeat(b_spatial[e].T, gw, axis=1)
            wsd = _row(jnp.repeat(w_spatial[e][:, 0, 0], gw))
            bsd = _row(jnp.repeat(b_spatial[e][:, 0], gw))
            shared = (_row(gmlp_ln_g[e]), _row(gmlp_ln_b[e]))
            convp = (conv_w[e], _row(conv_b[e]), _row(conv_ln_g[e]), _row(conv_ln_b[e]), wout)
            xp, buf_p = _even_prompt(xp, n_seq, gn, win, *shared, ws, bs_full, *convp, tm=tm)
            xs, vrows, buf_s = _even_sample(xs, state_conv[e], gn, win, *shared, wsd, bsd,
                                            *convp, bs=bs)
            conv_p.append(buf_p)
            conv_s.append(buf_s)
            v_s.append(vrows.reshape(n_dec, 1, aw))
            ffn_p = dict()
            ffn_s = dict()
        else:
            o = layer // 2
            pad = jnp.zeros((d, LANES - QK_ROPE), F32)
            win = jnp.concatenate([w_in_c[o], pad], axis=1).astype(BF16)
            wq = w_uq[o].reshape(q_lora, MLA_HEADS, QK_NOPE + QK_ROPE)
            wq = jnp.pad(wq, ((0, 0), (0, 0), (0, QK_PAD - QK_NOPE - QK_ROPE)))
            wq = wq.reshape(q_lora, MLA_HEADS * QK_PAD).astype(BF16)
            wuk = w_uk[o].reshape(kv_lora, MLA_HEADS * QK_NOPE).astype(BF16)
            wuvt = w_uv[o].reshape(kv_lora, MLA_HEADS * V_HEAD).T.astype(BF16)
            wukt = jnp.transpose(w_uk[o], (1, 2, 0)).astype(BF16)
            wuv_h = jnp.transpose(w_uv[o], (1, 0, 2)).astype(BF16)
            woc = w_out_c[o].astype(BF16)
            pw = (gn, win, _row(q_norm[o]), _row(kv_norm[o]), wq, wuk, wuvt)

            q, k, vt, ckv, kpe = _mla_proj(xp, *pw, cos_p, slo_p, shi_p, tm=tp, with_kv=True, tk=tq)
            attn_p = _flash(q, k, vt, n_seq, tq=tq)
            ckv_p.append(ckv.reshape(n_seq, seq, kv_lora))
            kpe_p.append(kpe.reshape(n_seq, seq, QK_ROPE))

            qs, ckv_n, kpe_n = _mla_proj(xs, *pw, cos_s, slo_s, shi_s, tm=n_dec, with_kv=False)
            q_lat = jnp.transpose(_absorb_q(qs, wukt), (1, 0, 2))
            q_pe = jnp.transpose(qs[:, :, QK_NOPE:QK_NOPE + QK_ROPE], (1, 0, 2))
            o_lat = _decode_attn(page_table, q_lat, q_pe,
                                 ckv_n.reshape(n_dec, 1, kv_lora), kpe_n.reshape(n_dec, 1, QK_ROPE),
                                 cache_ckv, cache_kpe_t, layer=o)
            attn_s = _value_up(jnp.transpose(o_lat, (1, 0, 2)), wuv_h)
            ckv_s.append(ckv_n.reshape(n_dec, 1, kv_lora))
            kpe_s.append(kpe_n.reshape(n_dec, 1, QK_ROPE))
            ffn_p = dict(o=attn_p, wo=woc)
            ffn_s = dict(o=attn_s, wo=woc)

        res_p = _ffn(xp, gf, wgu, wd, tm=tm, gf=nf, write_x=not last, **ffn_p)
        res_s = _ffn(xs, gf, wgu, wd, tm=n_dec, gf=nf, write_x=not last, **ffn_s)
        xp, xs = res_p[0], res_s[0]

    y_prompt = xp.reshape(n_seq, seq, d)
    y_sample = xs.reshape(n_dec, 1, d)
    return (y_prompt, y_sample, jnp.stack(ckv_p), jnp.stack(kpe_p), jnp.stack(ckv_s),
            jnp.stack(kpe_s), jnp.stack(conv_p), jnp.stack(conv_s), jnp.stack(v_s))
```

```python
import functools
import math

import jax
import jax.numpy as jnp
from jax import lax
from jax.experimental import pallas as pl
from jax.experimental.pallas import tpu as pltpu

F32 = jnp.float32
BF16 = jnp.bfloat16

EPS = 1e-6
CHUNK = 128
A_GROUPS = 4
CONV_WIDTH = 31
CONV_STATE = CONV_WIDTH - 1
MLA_HEADS = 8
QK_NOPE = 128
QK_ROPE = 64
V_HEAD = 128
ROPE_THETA = 10000.0
ATTN_SCALE = (QK_NOPE + QK_ROPE) ** -0.5
Q_SCALE = ATTN_SCALE * math.log2(math.e)

LANES = 128
SUBLANES = 8
HALO = 32
CONV_ROW_BLOCK = 64
QK_PAD = 2 * LANES
VMEM_LIMIT = 56 * 1024 * 1024


def _dot(a, b):
    return jnp.dot(a, b, preferred_element_type=F32)


def _dot_nt(a, b):
    return lax.dot_general(a, b, (((1,), (1,)), ((), ())), preferred_element_type=F32)


def _rms(x, g):
    return x * lax.rsqrt(jnp.mean(x * x, axis=-1, keepdims=True) + EPS) * g


def _ln(x, g, b):
    mu = jnp.mean(x, axis=-1, keepdims=True)
    xc = x - mu
    var = jnp.mean(xc * xc, axis=-1, keepdims=True)
    return xc * lax.rsqrt(var + EPS) * g + b


def _gelu(x):
    return 0.5 * x * (1.0 + jnp.tanh(math.sqrt(2.0 / math.pi) * (x + 0.044715 * (x * x * x))))


def _silu(x):
    return x * jax.nn.sigmoid(x)


def _const_spec(shape):
    nd = len(shape)
    return pl.BlockSpec(shape, lambda *_: (0,) * nd, pipeline_mode=pl.Buffered(1))


def _layer_spec(stacked, layer):
    tail = stacked.shape[1:]
    return pl.BlockSpec((None,) + tail, lambda *_: (layer,) + (0,) * len(tail),
                        pipeline_mode=pl.Buffered(1))


def _params(n_axes):
    return pltpu.CompilerParams(dimension_semantics=("arbitrary",) * n_axes,
                                vmem_limit_bytes=VMEM_LIMIT)


def _swiglu(x, gn_ref, wg_ref, wu_ref, wd_ref):
    h = _rms(x, gn_ref[...]).astype(BF16)
    a = (_silu(_dot(h, wg_ref[...])) * _dot(h, wu_ref[...])).astype(BF16)
    return x + _dot(a, wd_ref[...])


def _ffn_body(*refs, has_proj, has_final, write_x):
    it = iter(refs)
    x_ref = next(it)
    if has_proj:
        o_ref = next(it)
        wo_ref = next(it)
    gn_ref, wg_ref, wu_ref, wd_ref = next(it), next(it), next(it), next(it)
    if has_final:
        gf_ref = next(it)
    if write_x:
        out_ref = next(it)
    if has_final:
        yf_ref = next(it)

    x = x_ref[...]
    if has_proj:
        x = x + _dot(o_ref[...], wo_ref[...])
    xn = _swiglu(x, gn_ref, wg_ref, wu_ref, wd_ref)
    if write_x:
        out_ref[...] = xn
    if has_final:
        yf_ref[...] = _rms(xn, gf_ref[...])


def _ffn(x, layer, ffn_w, *, tm, o=None, wo=None, gf=None, write_x=True):
    rows, d = x.shape
    has_proj = o is not None
    has_final = gf is not None
    row_spec = pl.BlockSpec((tm, d), lambda i: (i, 0))
    args, specs = [x], [row_spec]
    if has_proj:
        args += [o, wo[0]]
        specs += [pl.BlockSpec((tm, o.shape[1]), lambda i: (i, 0)), _layer_spec(*wo)]
    args += list(ffn_w)
    specs += [_layer_spec(w, layer) for w in ffn_w]
    if has_final:
        args.append(gf)
        specs.append(_const_spec(gf.shape))
    out_shape, out_specs = [], []
    if write_x:
        out_shape.append(jax.ShapeDtypeStruct((rows, d), F32))
        out_specs.append(row_spec)
    if has_final:
        out_shape.append(jax.ShapeDtypeStruct((rows, d), F32))
        out_specs.append(row_spec)
    return pl.pallas_call(
        functools.partial(_ffn_body, has_proj=has_proj, has_final=has_final, write_x=write_x),
        grid=(rows // tm,),
        in_specs=specs,
        out_specs=out_specs,
        out_shape=out_shape,
        compiler_params=_params(1),
        name="ffn",
    )(*args)


def _even_layer_body(x_ref, gn_ref, win_ref, lng_ref, lnb_ref, ws_ref, bs_ref,
                     cw_ref, cb_ref, clg_ref, clb_ref, wout_ref,
                     gf_ref, wg_ref, wu_ref, wd_ref,
                     out_ref, conv_ref, xe_ref, cv_ref, ab_ref, x1_ref, *, tm, tiles_per_seq, aw, bw):
    i = pl.program_id(0)

    @pl.when(i == 0)
    def _():
        x1_ref[1] = jnp.zeros(x1_ref.shape[1:], F32)

    @pl.when(i % tiles_per_seq == 0)
    def _():
        xe_ref[0:HALO, :] = jnp.zeros((HALO, bw), F32)

    x_prev = x1_ref[(i + 1) % 2]
    x = x_ref[...]
    h = _rms(x, gn_ref[...]).astype(BF16)

    u = _gelu(_dot(h, win_ref[:, 0:aw]))
    v = _ln(_gelu(_dot(h, win_ref[:, aw:2 * aw])), lng_ref[...], lnb_ref[...])
    vb = v.astype(BF16)
    gw = aw // A_GROUPS
    for c in range(tm // CHUNK):
        r = slice(c * CHUNK, (c + 1) * CHUNK)
        for g in range(A_GROUPS):
            cs = slice(g * gw, (g + 1) * gw)
            s = _dot(ws_ref[g], vb[r, cs]) + bs_ref[:, cs]
            ab_ref[r, cs] = (u[r, cs] * s).astype(BF16)

    zb = _dot(h, win_ref[:, 2 * aw:2 * aw + bw])
    zg = _dot(h, win_ref[:, 2 * aw + bw:])
    xe_ref[HALO:HALO + tm, :] = zb * jax.nn.sigmoid(zg)

    off = HALO - CONV_STATE
    rb = min(tm, CONV_ROW_BLOCK)
    for r0 in range(0, tm, rb):
        for c0 in range(0, bw, LANES):
            cs = slice(c0, c0 + LANES)
            acc = jnp.broadcast_to(cb_ref[:, cs], (rb, LANES))
            for r in range(SUBLANES):
                taps = [a for a in range(HALO // SUBLANES + 1)
                        if 0 <= a * SUBLANES + r - off < CONV_WIDTH]
                lo = r0 + taps[0] * SUBLANES
                span = rb + (taps[-1] - taps[0]) * SUBLANES
                if r == 0:
                    win = xe_ref[lo:lo + span, cs]
                else:
                    win = pltpu.roll(xe_ref[lo:lo + span + SUBLANES, cs], span + SUBLANES - r, 0)
                for a in taps:
                    k = a * SUBLANES + r - off
                    sh = (a - taps[0]) * SUBLANES
                    acc = acc + cw_ref[k:k + 1, cs] * win[sh:sh + rb]
            cv_ref[r0:r0 + rb, cs] = acc
    ab_ref[:, aw:aw + bw] = _silu(_ln(cv_ref[...], clg_ref[...], clb_ref[...])).astype(BF16)

    conv_ref[0] = xe_ref[tm + off:tm + HALO, :]
    xe_ref[0:HALO, :] = xe_ref[tm:tm + HALO, :]
    x1 = x + _dot(ab_ref[...], wout_ref[...])

    out_ref[...] = _swiglu(x_prev, gf_ref, wg_ref, wu_ref, wd_ref)
    x1_ref[i % 2] = x1


def _even_layer_prompt(x, n_seq, layer, mix_w, ffn_w, *, tm):
    rows, d = x.shape
    aw = mix_w[2].shape[-1]
    bw = mix_w[8].shape[-1]
    n_tiles = rows // tm
    tiles_per_seq = n_tiles // n_seq
    last = n_tiles - 1
    return pl.pallas_call(
        functools.partial(_even_layer_body, tm=tm, tiles_per_seq=tiles_per_seq, aw=aw, bw=bw),
        grid=(n_tiles + 1,),
        in_specs=([pl.BlockSpec((tm, d), lambda i: (jnp.minimum(i, last), 0))]
                  + [_layer_spec(w, layer // 2) for w in mix_w]
                  + [_layer_spec(w, layer) for w in ffn_w]),
        out_specs=[pl.BlockSpec((tm, d), lambda i: (jnp.maximum(i - 1, 0), 0)),
                   pl.BlockSpec((1, CONV_STATE, bw),
                                lambda i: (jnp.minimum(i, last) // tiles_per_seq, 0, 0))],
        out_shape=[jax.ShapeDtypeStruct((rows, d), F32),
                   jax.ShapeDtypeStruct((n_seq, CONV_STATE, bw), F32)],
        scratch_shapes=[pltpu.VMEM((HALO + tm, bw), F32),
                        pltpu.VMEM((tm, bw), F32),
                        pltpu.VMEM((tm, aw + bw), BF16),
                        pltpu.VMEM((2, tm, d), F32)],
        compiler_params=_params(1),
        name="even_layer",
    )(x, *mix_w, *ffn_w)


def _even_sample_body(x_ref, st_ref, gn_ref, win_ref, lng_ref, lnb_ref, wsd_ref, bsd_ref,
                      cw_ref, cb_ref, clg_ref, clb_ref, wout_ref,
                      out_ref, v_ref, nst_ref, *, aw, bw):
    x = x_ref[...]
    h = _rms(x, gn_ref[...]).astype(BF16)
    u = _gelu(_dot(h, win_ref[:, 0:aw]))
    v = _ln(_gelu(_dot(h, win_ref[:, aw:2 * aw])), lng_ref[...], lnb_ref[...])
    v_ref[...] = v
    s = v * wsd_ref[...] + bsd_ref[...]
    a_out = u * s

    zb = _dot(h, win_ref[:, 2 * aw:2 * aw + bw])
    zg = _dot(h, win_ref[:, 2 * aw + bw:])
    glu = zb * jax.nn.sigmoid(zg)

    c = glu * cw_ref[CONV_STATE:CONV_WIDTH, :] + cb_ref[...]
    for k in range(CONV_STATE):
        c = c + st_ref[k] * cw_ref[k:k + 1, :]
        if k > 0:
            nst_ref[k - 1] = st_ref[k]
    nst_ref[CONV_STATE - 1] = glu
    b_out = _silu(_ln(c, clg_ref[...], clb_ref[...]))

    ab = jnp.concatenate([a_out.astype(BF16), b_out.astype(BF16)], axis=-1)
    out_ref[...] = x + _dot(ab, wout_ref[...])


def _even_sample(x, state_t, layer, mix_w, *, bs):
    rows, d = x.shape
    aw = mix_w[2].shape[-1]
    bw = mix_w[8].shape[-1]
    row_spec = pl.BlockSpec((bs, d), lambda i: (i, 0))
    return pl.pallas_call(
        functools.partial(_even_sample_body, aw=aw, bw=bw),
        grid=(rows // bs,),
        in_specs=([row_spec,
                   pl.BlockSpec((None, CONV_STATE, bs, bw), lambda i: (layer // 2, 0, i, 0))]
                  + [_layer_spec(w, layer // 2) for w in mix_w]),
        out_specs=[row_spec, pl.BlockSpec((bs, aw), lambda i: (i, 0)),
                   pl.BlockSpec((CONV_STATE, bs, bw), lambda i: (0, i, 0))],
        out_shape=[jax.ShapeDtypeStruct((rows, d), F32),
                   jax.ShapeDtypeStruct((rows, aw), F32),
                   jax.ShapeDtypeStruct((CONV_STATE, rows, bw), F32)],
        compiler_params=_params(1),
        name="even_sample",
    )(x, state_t, *mix_w)


def _rope(t, cos, sin_lo, sin_hi):
    half = QK_ROPE // 2
    return (t * cos + pltpu.roll(t, LANES - half, 1) * sin_lo
            + pltpu.roll(t, half, 1) * sin_hi)


def _mla_proj_body(*refs, with_kv, q_lora, kv_lora):
    (x_ref, gn_ref, win_ref, qn_ref, kvn_ref, wuq_ref, wuk_ref, wuvt_ref,
     cos_ref, slo_ref, shi_ref) = refs[:11]
    if with_kv:
        q_ref, k_ref, vt_ref, ckv_ref, kpe_ref = refs[11:]
    else:
        q_ref, ckv_ref, kpe_ref = refs[11:]

    x = x_ref[...]
    h = _rms(x, gn_ref[...]).astype(BF16)
    z = _dot(h, win_ref[...])
    cq = _rms(z[:, 0:q_lora], qn_ref[...])
    ckv = _rms(z[:, q_lora:q_lora + kv_lora], kvn_ref[...])
    cos, slo, shi = cos_ref[...], slo_ref[...], shi_ref[...]
    kr = _rope(z[:, q_lora + kv_lora:], cos, slo, shi)
    ckv_ref[...] = ckv
    kpe_ref[...] = kr[:, 0:QK_ROPE]

    q = _dot(cq.astype(BF16), wuq_ref[...])
    for hh in range(MLA_HEADS):
        base = hh * QK_PAD
        q_ref[hh, :, 0:QK_NOPE] = (q[:, base:base + QK_NOPE] * Q_SCALE).astype(BF16)
        qp = _rope(q[:, base + QK_NOPE:base + QK_PAD], cos, slo, shi)
        q_ref[hh, :, QK_NOPE:QK_PAD] = (qp * Q_SCALE).astype(BF16)

    if with_kv:
        cb = ckv.astype(BF16)
        kn = _dot(cb, wuk_ref[...])
        vt = _dot_nt(wuvt_ref[...], cb)
        krb = kr.astype(BF16)
        for hh in range(MLA_HEADS):
            k_ref[hh, :, 0:QK_NOPE] = kn[:, hh * QK_NOPE:(hh + 1) * QK_NOPE].astype(BF16)
            k_ref[hh, :, QK_NOPE:QK_PAD] = krb
            vt_ref[hh, 0] = vt[hh * V_HEAD:(hh + 1) * V_HEAD, :].astype(BF16)


def _mla_proj(x, odd, proj_w, cos, slo, shi, *, tm, with_kv, tk=None):
    rows, d = x.shape
    q_lora = proj_w[2].shape[-1]
    kv_lora = proj_w[3].shape[-1]
    n_pos_tiles = cos.shape[0] // tm
    row_spec = pl.BlockSpec((tm, d), lambda i: (i, 0))
    pos_spec = pl.BlockSpec((tm, LANES), lambda i: (i % n_pos_tiles, 0))
    head_spec = pl.BlockSpec((MLA_HEADS, tm, QK_PAD), lambda i: (0, i, 0))
    out_specs = [head_spec]
    out_shape = [jax.ShapeDtypeStruct((MLA_HEADS, rows, QK_PAD), BF16)]
    if with_kv:
        per_tile = tk // tm
        out_specs += [head_spec,
                      pl.BlockSpec((MLA_HEADS, 1, V_HEAD, tm),
                                   lambda i: (0, i // per_tile, 0, i % per_tile))]
        out_shape += [jax.ShapeDtypeStruct((MLA_HEADS, rows, QK_PAD), BF16),
                      jax.ShapeDtypeStruct((MLA_HEADS, rows // tk, V_HEAD, tk), BF16)]
    out_specs += [pl.BlockSpec((tm, kv_lora), lambda i: (i, 0)),
                  pl.BlockSpec((tm, QK_ROPE), lambda i: (i, 0))]
    out_shape += [jax.ShapeDtypeStruct((rows, kv_lora), F32),
                  jax.ShapeDtypeStruct((rows, QK_ROPE), F32)]
    return pl.pallas_call(
        functools.partial(_mla_proj_body, with_kv=with_kv, q_lora=q_lora, kv_lora=kv_lora),
        grid=(rows // tm,),
        in_specs=[row_spec] + [_layer_spec(w, odd) for w in proj_w] + [pos_spec] * 3,
        out_specs=out_specs,
        out_shape=out_shape,
        compiler_params=_params(1),
        name="mla_proj",
    )(x, *proj_w, cos, slo, shi)


def _flash_body(q_ref, k_ref, vt_ref, o_ref, st_ref, mt_ref, m_ref, l_ref, acc_ref, *, tq):
    def tile(ref, t):
        return ref[0, pl.ds(pl.multiple_of(t * tq, tq), tq), :]

    def q_tile(qi, carry):
        q = tile(q_ref, qi)
        m_ref[...] = jnp.full(m_ref.shape, -jnp.inf, F32)
        l_ref[...] = jnp.zeros(l_ref.shape, F32)
        acc_ref[...] = jnp.zeros(acc_ref.shape, F32)

        def scores(j):
            return _dot_nt(tile(k_ref, j), q)

        def consume(j, st, mt):
            m_prev = m_ref[...]
            m_new = jnp.maximum(m_prev, mt)
            alpha = jnp.exp2(m_prev - m_new)
            pt = jnp.exp2(st - m_new)
            l_ref[...] = alpha * l_ref[...] + jnp.sum(pt, axis=0, keepdims=True)
            acc_ref[...] = alpha * acc_ref[...] + _dot(vt_ref[0, j], pt.astype(BF16))
            m_ref[...] = m_new

        st0 = scores(0)
        st_ref[...] = st0
        mt_ref[...] = jnp.max(st0, axis=0, keepdims=True)

        def kv_tile(j, c):
            st = st_ref[...]
            mt = mt_ref[...]
            st_next = scores(j + 1)
            consume(j, st, mt)
            st_ref[...] = st_next
            mt_ref[...] = jnp.max(st_next, axis=0, keepdims=True)
            return c

        lax.fori_loop(0, qi, kv_tile, 0)
        key = lax.broadcasted_iota(jnp.int32, (tq, tq), 0)
        qry = lax.broadcasted_iota(jnp.int32, (tq, tq), 1)
        st = jnp.where(key <= qry, st_ref[...], -jnp.inf)
        consume(qi, st, jnp.max(st, axis=0, keepdims=True))
        o_ref[pl.ds(pl.multiple_of(qi * tq, tq), tq), :] = (acc_ref[...] / l_ref[...]).T.astype(BF16)
        return carry

    lax.fori_loop(0, q_ref.shape[1] // tq, q_tile, 0)


def _flash(q, k, vt, n_seq, *, tq):
    heads, rows, _ = q.shape
    seq = rows // n_seq
    qk_spec = pl.BlockSpec((1, seq, QK_PAD), lambda b, h: (h, b, 0))
    return pl.pallas_call(
        functools.partial(_flash_body, tq=tq),
        grid=(n_seq, heads),
        in_specs=[qk_spec, qk_spec,
                  pl.BlockSpec((1, seq // tq, V_HEAD, tq), lambda b, h: (h, b, 0, 0))],
        out_specs=pl.BlockSpec((seq, V_HEAD), lambda b, h: (b, h)),
        out_shape=jax.ShapeDtypeStruct((rows, heads * V_HEAD), BF16),
        scratch_shapes=[pltpu.VMEM((tq, tq), F32), pltpu.VMEM((1, tq), F32),
                        pltpu.VMEM((1, tq), F32), pltpu.VMEM((1, tq), F32),
                        pltpu.VMEM((V_HEAD, tq), F32)],
        compiler_params=_params(2),
        name="flash",
    )(q, k, vt)


def _absorb_q_body(q_ref, wukt_ref, o_ref):
    o_ref[0] = _dot(q_ref[0, :, 0:QK_NOPE], wukt_ref[0]).astype(BF16)


def _absorb_q(q, odd, wukt):
    heads, rows, _ = q.shape
    kv_lora = wukt.shape[-1]
    return pl.pallas_call(
        _absorb_q_body,
        grid=(heads,),
        in_specs=[pl.BlockSpec((1, rows, QK_PAD), lambda h: (h, 0, 0)),
                  pl.BlockSpec((None, 1, QK_NOPE, kv_lora), lambda h: (odd, h, 0, 0))],
        out_specs=pl.BlockSpec((1, rows, kv_lora), lambda h: (h, 0, 0)),
        out_shape=jax.ShapeDtypeStruct((heads, rows, kv_lora), BF16),
        compiler_params=_params(1),
        name="absorb_q",
    )(q, wukt)


def _page_copy(pt_ref, ckv_hbm, kpe_hbm, ckv_buf, kpe_buf, sem, layer, seq, slot, j, page):
    pid = pt_ref[seq, j]
    keys = pl.ds(j * page, page)
    return (pltpu.make_async_copy(ckv_hbm.at[layer, pid], ckv_buf.at[slot, keys, :], sem.at[0, slot]),
            pltpu.make_async_copy(kpe_hbm.at[layer, pid], kpe_buf.at[slot, :, keys], sem.at[1, slot]))


def _decode_body(pt_ref, ql_ref, qp_ref, cn_ref, kn_ref, ckv_hbm, kpe_hbm, o_ref,
                 ckv_buf, kpe_buf, s_ref, sem, *, layer, n_pages, page, kc):
    b = pl.program_id(0)
    nb = pl.num_programs(0)
    slot = b % 2
    copy = functools.partial(_page_copy, pt_ref, ckv_hbm, kpe_hbm, ckv_buf, kpe_buf, sem, layer)

    def start_all(seq, slt):
        for j in range(n_pages):
            c0, c1 = copy(seq, slt, j, page)
            c0.start()
            c1.start()

    @pl.when(b == 0)
    def _():
        start_all(0, 0)

    @pl.when(b + 1 < nb)
    def _():
        start_all(b + 1, 1 - slot)

    def wait_one(j, carry):
        c0, c1 = copy(b, slot, j, page)
        c0.wait()
        c1.wait()
        return carry

    lax.fori_loop(0, n_pages, wait_one, 0)

    ql = ql_ref[0]
    qp = qp_ref[0]
    past = n_pages * page
    for c in range(past // kc):
        keys = pl.ds(c * kc, kc)
        s_ref[:, c * kc:(c + 1) * kc] = (
            _dot_nt(ql, ckv_buf[slot, keys, :].astype(BF16))
            + _dot(qp, kpe_buf[slot, :, keys].astype(BF16)))

    cn = cn_ref[0]
    kn = kn_ref[0]
    s_new = (jnp.sum(ql.astype(F32) * cn, axis=-1, keepdims=True)
             + jnp.sum(qp.astype(F32) * kn, axis=-1, keepdims=True))
    s = s_ref[...]
    m = jnp.maximum(jnp.max(s, axis=-1, keepdims=True), s_new)
    p = jnp.exp2(s - m)
    p_new = jnp.exp2(s_new - m)
    denom = jnp.sum(p, axis=-1, keepdims=True) + p_new
    pb = p.astype(BF16)
    acc = p_new * cn
    for c in range(past // kc):
        keys = pl.ds(c * kc, kc)
        acc = acc + _dot(pb[:, c * kc:(c + 1) * kc], ckv_buf[slot, keys, :].astype(BF16))
    o_ref[0] = acc / denom


def _decode_attn(page_table, q_lat, q_pe, ckv_new, kpe_new, cache_ckv, cache_kpe_t, *, layer):
    n_seq, heads, kv_lora = q_lat.shape
    n_pages = page_table.shape[1]
    page = cache_ckv.shape[2]
    past = n_pages * page
    kc = min(past, 1024)
    grid_spec = pltpu.PrefetchScalarGridSpec(
        num_scalar_prefetch=1,
        grid=(n_seq,),
        in_specs=[pl.BlockSpec((1, heads, kv_lora), lambda b, pt: (b, 0, 0)),
                  pl.BlockSpec((1, heads, QK_ROPE), lambda b, pt: (b, 0, 0)),
                  pl.BlockSpec((1, 1, kv_lora), lambda b, pt: (b, 0, 0)),
                  pl.BlockSpec((1, 1, QK_ROPE), lambda b, pt: (b, 0, 0)),
                  pl.BlockSpec(memory_space=pl.ANY),
                  pl.BlockSpec(memory_space=pl.ANY)],
        out_specs=pl.BlockSpec((1, heads, kv_lora), lambda b, pt: (b, 0, 0)),
        scratch_shapes=[pltpu.VMEM((2, past, kv_lora), F32),
                        pltpu.VMEM((2, QK_ROPE, past), F32),
                        pltpu.VMEM((heads, past), F32),
                        pltpu.SemaphoreType.DMA((2, 2))],
    )
    return pl.pallas_call(
        functools.partial(_decode_body, layer=layer, n_pages=n_pages, page=page, kc=kc),
        grid_spec=grid_spec,
        out_shape=jax.ShapeDtypeStruct((n_seq, heads, kv_lora), F32),
        compiler_params=_params(1),
        name="decode_attn",
    )(page_table, q_lat, q_pe, ckv_new, kpe_new, cache_ckv, cache_kpe_t)


def _value_up_body(ol_ref, wuv_ref, o_ref):
    o_ref[...] = _dot(ol_ref[0].astype(BF16), wuv_ref[0]).astype(BF16)


def _value_up(o_lat, odd, wuv_h):
    heads, rows, kv_lora = o_lat.shape
    return pl.pallas_call(
        _value_up_body,
        grid=(heads,),
        in_specs=[pl.BlockSpec((1, rows, kv_lora), lambda h: (h, 0, 0)),
                  pl.BlockSpec((None, 1, kv_lora, V_HEAD), lambda h: (odd, h, 0, 0))],
        out_specs=pl.BlockSpec((rows, V_HEAD), lambda h: (0, h)),
        out_shape=jax.ShapeDtypeStruct((rows, heads * V_HEAD), BF16),
        compiler_params=_params(1),
        name="value_up",
    )(o_lat, wuv_h)


def _rope_tables(pos):
    half = QK_ROPE // 2
    inv = ROPE_THETA ** (-jnp.arange(half, dtype=F32) / half)
    ang = pos.astype(F32)[:, None] * inv[None, :]
    c, s = jnp.cos(ang), jnp.sin(ang)
    z = jnp.zeros_like(c)
    zz = jnp.zeros((pos.shape[0], LANES - QK_ROPE), F32)
    cos = jnp.concatenate([c, c, zz], axis=1)
    sin_lo = jnp.concatenate([-s, z, zz], axis=1)
    sin_hi = jnp.concatenate([z, s, zz], axis=1)
    return cos, sin_lo, sin_hi


def _row(v):
    return v.reshape(1, -1)


def kernel(x_prompt, x_sample, cache_ckv, cache_kpe, page_table, state_conv, norm_mix, norm_ffn, norm_final, w_in_ab, gmlp_ln_g, gmlp_ln_b, w_spatial, b_spatial, conv_w, conv_b, conv_ln_g, conv_ln_b, w_out_ab, w_in_c, q_norm, kv_norm, w_uq, w_uk, w_uv, w_out_c, w_gate, w_up, w_down):
    n_seq, seq, d = x_prompt.shape
    n_dec, dec_seq, _ = x_sample.shape
    assert dec_seq == 1, "sample group handles one new token per sequence"
    depth = norm_mix.shape[0]
    aw = gmlp_ln_g.shape[1]
    q_lora = q_norm.shape[1]
    kv_lora = kv_norm.shape[1]
    past = page_table.shape[1] * cache_ckv.shape[2]
    assert past % CHUNK == 0 and seq % CHUNK == 0

    tm = min(256, seq)
    tf = min(512, seq)
    tp = min(512, seq)
    tq = min(1024, seq)
    bs = min(32, n_dec)

    xp = x_prompt.reshape(n_seq * seq, d)
    xs = x_sample.reshape(n_dec, d)
    cache_kpe_t = jnp.swapaxes(cache_kpe, 2, 3)

    tril = jnp.tril(jnp.ones((CHUNK, CHUNK), dtype=bool))
    cos_p, slo_p, shi_p = _rope_tables(jnp.arange(seq))
    cos_s, slo_s, shi_s = _rope_tables(jnp.full((n_dec,), past))

    n_even, n_odd = w_in_ab.shape[0], w_in_c.shape[0]
    gw = aw // A_GROUPS

    def rows3(p):
        return p[:, None, :]

    ffn_w = (rows3(norm_ffn), w_gate.astype(BF16), w_up.astype(BF16), w_down.astype(BF16))
    ws_tril = jnp.where(tril[None, None], w_spatial, 0.0).astype(BF16)
    bs_full = jnp.repeat(jnp.swapaxes(b_spatial, 1, 2), gw, axis=2)
    mix_tail = (conv_w, rows3(conv_b), rows3(conv_ln_g), rows3(conv_ln_b), w_out_ab.astype(BF16))
    mix_head = (rows3(norm_mix[0::2]), w_in_ab.astype(BF16), rows3(gmlp_ln_g), rows3(gmlp_ln_b))
    mix_w = mix_head + (ws_tril, bs_full) + mix_tail
    mix_w_s = mix_head + (rows3(jnp.repeat(w_spatial[:, :, 0, 0], gw, axis=1)),
                          rows3(jnp.repeat(b_spatial[:, :, 0], gw, axis=1))) + mix_tail
    state_t = jnp.swapaxes(state_conv, 1, 2)

    win_c = jnp.pad(w_in_c, ((0, 0), (0, 0), (0, LANES - QK_ROPE))).astype(BF16)
    wq = w_uq.reshape(n_odd, q_lora, MLA_HEADS, QK_NOPE + QK_ROPE)
    wq = jnp.pad(wq, ((0, 0), (0, 0), (0, 0), (0, QK_PAD - QK_NOPE - QK_ROPE)))
    wq = wq.reshape(n_odd, q_lora, MLA_HEADS * QK_PAD).astype(BF16)
    wuk = w_uk.reshape(n_odd, kv_lora, MLA_HEADS * QK_NOPE).astype(BF16)
    wuvt = jnp.swapaxes(w_uv.reshape(n_odd, kv_lora, MLA_HEADS * V_HEAD), 1, 2).astype(BF16)
    wukt = jnp.transpose(w_uk, (0, 2, 3, 1)).astype(BF16)
    wuv_h = jnp.transpose(w_uv, (0, 2, 1, 3)).astype(BF16)
    woc = w_out_c.astype(BF16)
    proj_w = (rows3(norm_mix[1::2]), win_c, rows3(q_norm), rows3(kv_norm), wq, wuk, wuvt)

    ckv_p, kpe_p, ckv_s, kpe_s, conv_p, conv_s, v_s = [], [], [], [], [], [], []
    for layer in range(depth):
        last = layer == depth - 1
        nf = _row(norm_final) if last else None
        if layer % 2 == 0:
            xs, vrows, buf_s = _even_sample(xs, state_t, layer, mix_w_s, bs=bs)
            (xs,) = _ffn(xs, layer, ffn_w, tm=n_dec)
            xp, buf_p = _even_layer_prompt(xp, n_seq, layer, mix_w, ffn_w, tm=tm)
            conv_p.append(buf_p)
            conv_s.append(buf_s)
            v_s.append(vrows.reshape(n_dec, 1, aw))
        else:
            o = layer // 2
            q, k, vt, ckv, kpe = _mla_proj(xp, o, proj_w, cos_p, slo_p, shi_p,
                                           tm=tp, with_kv=True, tk=tq)
            attn_p = _flash(q, k, vt, n_seq, tq=tq)
            ckv_p.append(ckv.reshape(n_seq, seq, kv_lora))
            kpe_p.append(kpe.reshape(n_seq, seq, QK_ROPE))

            qs, ckv_n, kpe_n = _mla_proj(xs, o, proj_w, cos_s, slo_s, shi_s, tm=n_dec, with_kv=False)
            q_lat = jnp.transpose(_absorb_q(qs, o, wukt), (1, 0, 2))
            q_pe = jnp.transpose(qs[:, :, QK_NOPE:QK_NOPE + QK_ROPE], (1, 0, 2))
            o_lat = _decode_attn(page_table, q_lat, q_pe,
                                 ckv_n.reshape(n_dec, 1, kv_lora), kpe_n.reshape(n_dec, 1, QK_ROPE),
                                 cache_ckv, cache_kpe_t, layer=o)
            attn_s = _value_up(jnp.transpose(o_lat, (1, 0, 2)), o, wuv_h)
            ckv_s.append(ckv_n.reshape(n_dec, 1, kv_lora))
            kpe_s.append(kpe_n.reshape(n_dec, 1, QK_ROPE))

            res_p = _ffn(xp, layer, ffn_w, tm=tf, o=attn_p, wo=(woc, o), gf=nf, write_x=not last)
            res_s = _ffn(xs, layer, ffn_w, tm=n_dec, o=attn_s, wo=(woc, o), gf=nf, write_x=not last)
            xp, xs = res_p[0], res_s[0]
    assert depth % 2 == 0, "the final RMSNorm is fused into the last (odd) layer's FFN"

    y_prompt = xp.reshape(n_seq, seq, d)
    y_sample = xs.reshape(n_dec, 1, d)
    return (y_prompt, y_sample, jnp.stack(ckv_p), jnp.stack(kpe_p), jnp.stack(ckv_s),
            jnp.stack(kpe_s), jnp.stack(conv_p), jnp.swapaxes(jnp.stack(conv_s), 1, 2),
            jnp.stack(v_s))
```

```python
import functools
import math

import jax
import jax.numpy as jnp
from jax import lax
from jax.experimental import pallas as pl
from jax.experimental.pallas import tpu as pltpu

F32 = jnp.float32
BF16 = jnp.bfloat16

EPS = 1e-6
CHUNK = 128
A_GROUPS = 4
CONV_WIDTH = 31
CONV_STATE = CONV_WIDTH - 1
MLA_HEADS = 8
QK_NOPE = 128
QK_ROPE = 64
V_HEAD = 128
ROPE_THETA = 10000.0
ATTN_SCALE = (QK_NOPE + QK_ROPE) ** -0.5
Q_SCALE = ATTN_SCALE * math.log2(math.e)

LANES = 128
SUBLANES = 8
HALO = 32
CONV_ROW_BLOCK = 64
KV_CHUNK = 256
QK_PAD = 2 * LANES
VMEM_LIMIT = 56 * 1024 * 1024


def _dot(a, b):
    return jnp.dot(a, b, preferred_element_type=F32)


def _dot_nt(a, b):
    return lax.dot_general(a, b, (((1,), (1,)), ((), ())), preferred_element_type=F32)


def _rms(x, g):
    return x * lax.rsqrt(jnp.mean(x * x, axis=-1, keepdims=True) + EPS) * g


def _ln(x, g, b):
    mu = jnp.mean(x, axis=-1, keepdims=True)
    xc = x - mu
    var = jnp.mean(xc * xc, axis=-1, keepdims=True)
    return xc * lax.rsqrt(var + EPS) * g + b


def _gelu(x):
    return 0.5 * x * (1.0 + jnp.tanh(math.sqrt(2.0 / math.pi) * (x + 0.044715 * (x * x * x))))


def _silu(x):
    return x * jax.nn.sigmoid(x)


def _const_spec(shape):
    nd = len(shape)
    return pl.BlockSpec(shape, lambda *_: (0,) * nd, pipeline_mode=pl.Buffered(1))


def _layer_spec(stacked, layer):
    tail = stacked.shape[1:]
    return pl.BlockSpec((None,) + tail, lambda *_: (layer,) + (0,) * len(tail),
                        pipeline_mode=pl.Buffered(1))


def _params(n_axes):
    return pltpu.CompilerParams(dimension_semantics=("arbitrary",) * n_axes,
                                vmem_limit_bytes=VMEM_LIMIT)


def _swiglu(x, gn_ref, wg_ref, wu_ref, wd_ref):
    h = _rms(x, gn_ref[...]).astype(BF16)
    a = (_silu(_dot(h, wg_ref[...])) * _dot(h, wu_ref[...])).astype(BF16)
    return x + _dot(a, wd_ref[...])


def _ffn_body(*refs, has_proj, has_final, write_x):
    it = iter(refs)
    x_ref = next(it)
    if has_proj:
        o_ref = next(it)
        wo_ref = next(it)
    gn_ref, wg_ref, wu_ref, wd_ref = next(it), next(it), next(it), next(it)
    if has_final:
        gf_ref = next(it)
    if write_x:
        out_ref = next(it)
    if has_final:
        yf_ref = next(it)

    x = x_ref[...]
    if has_proj:
        x = x + _dot(o_ref[...], wo_ref[...])
    xn = _swiglu(x, gn_ref, wg_ref, wu_ref, wd_ref)
    if write_x:
        out_ref[...] = xn
    if has_final:
        yf_ref[...] = _rms(xn, gf_ref[...])


def _ffn(x, layer, ffn_w, *, tm, o=None, wo=None, gf=None, write_x=True):
    rows, d = x.shape
    has_proj = o is not None
    has_final = gf is not None
    row_spec = pl.BlockSpec((tm, d), lambda i: (i, 0))
    args, specs = [x], [row_spec]
    if has_proj:
        args += [o, wo[0]]
        specs += [pl.BlockSpec((tm, o.shape[1]), lambda i: (i, 0)), _layer_spec(*wo)]
    args += list(ffn_w)
    specs += [_layer_spec(w, layer) for w in ffn_w]
    if has_final:
        args.append(gf)
        specs.append(_const_spec(gf.shape))
    out_shape, out_specs = [], []
    if write_x:
        out_shape.append(jax.ShapeDtypeStruct((rows, d), F32))
        out_specs.append(row_spec)
    if has_final:
        out_shape.append(jax.ShapeDtypeStruct((rows, d), F32))
        out_specs.append(row_spec)
    return pl.pallas_call(
        functools.partial(_ffn_body, has_proj=has_proj, has_final=has_final, write_x=write_x),
        grid=(rows // tm,),
        in_specs=specs,
        out_specs=out_specs,
        out_shape=out_shape,
        compiler_params=_params(1),
        name="ffn",
    )(*args)


def _even_layer_body(x_ref, gn_ref, win_ref, lng_ref, lnb_ref, ws_ref, bs_ref,
                     cw_ref, cb_ref, clg_ref, clb_ref, wout_ref,
                     gf_ref, wg_ref, wu_ref, wd_ref,
                     out_ref, conv_ref, xe_ref, cv_ref, ab_ref, x1_ref, *, tm, tiles_per_seq, aw, bw):
    i = pl.program_id(0)

    @pl.when(i == 0)
    def _():
        x1_ref[1] = jnp.zeros(x1_ref.shape[1:], F32)

    @pl.when(i % tiles_per_seq == 0)
    def _():
        xe_ref[0:HALO, :] = jnp.zeros((HALO, bw), F32)

    x_prev = x1_ref[(i + 1) % 2]
    x = x_ref[...]
    h = _rms(x, gn_ref[...]).astype(BF16)

    u = _gelu(_dot(h, win_ref[:, 0:aw]))
    v = _ln(_gelu(_dot(h, win_ref[:, aw:2 * aw])), lng_ref[...], lnb_ref[...])
    vb = v.astype(BF16)
    gw = aw // A_GROUPS
    for c in range(tm // CHUNK):
        r = slice(c * CHUNK, (c + 1) * CHUNK)
        for g in range(A_GROUPS):
            cs = slice(g * gw, (g + 1) * gw)
            s = _dot(ws_ref[g], vb[r, cs]) + bs_ref[:, cs]
            ab_ref[r, cs] = (u[r, cs] * s).astype(BF16)

    zb = _dot(h, win_ref[:, 2 * aw:2 * aw + bw])
    zg = _dot(h, win_ref[:, 2 * aw + bw:])
    xe_ref[HALO:HALO + tm, :] = zb * jax.nn.sigmoid(zg)

    off = HALO - CONV_STATE
    rb = min(tm, CONV_ROW_BLOCK)
    for r0 in range(0, tm, rb):
        for c0 in range(0, bw, LANES):
            cs = slice(c0, c0 + LANES)
            acc = jnp.broadcast_to(cb_ref[:, cs], (rb, LANES))
            for r in range(SUBLANES):
                taps = [a for a in range(HALO // SUBLANES + 1)
                        if 0 <= a * SUBLANES + r - off < CONV_WIDTH]
                lo = r0 + taps[0] * SUBLANES
                span = rb + (taps[-1] - taps[0]) * SUBLANES
                if r == 0:
                    win = xe_ref[lo:lo + span, cs]
                else:
                    win = pltpu.roll(xe_ref[lo:lo + span + SUBLANES, cs], span + SUBLANES - r, 0)
                for a in taps:
                    k = a * SUBLANES + r - off
                    sh = (a - taps[0]) * SUBLANES
                    acc = acc + cw_ref[k:k + 1, cs] * win[sh:sh + rb]
            cv_ref[r0:r0 + rb, cs] = acc
    ab_ref[:, aw:aw + bw] = _silu(_ln(cv_ref[...], clg_ref[...], clb_ref[...])).astype(BF16)

    conv_ref[0] = xe_ref[tm + off:tm + HALO, :]
    xe_ref[0:HALO, :] = xe_ref[tm:tm + HALO, :]
    x1 = x + _dot(ab_ref[...], wout_ref[...])

    out_ref[...] = _swiglu(x_prev, gf_ref, wg_ref, wu_ref, wd_ref)
    x1_ref[i % 2] = x1


def _even_layer_prompt(x, n_seq, layer, mix_w, ffn_w, *, tm):
    rows, d = x.shape
    aw = mix_w[2].shape[-1]
    bw = mix_w[8].shape[-1]
    n_tiles = rows // tm
    tiles_per_seq = n_tiles // n_seq
    last = n_tiles - 1
    return pl.pallas_call(
        functools.partial(_even_layer_body, tm=tm, tiles_per_seq=tiles_per_seq, aw=aw, bw=bw),
        grid=(n_tiles + 1,),
        in_specs=([pl.BlockSpec((tm, d), lambda i: (jnp.minimum(i, last), 0))]
                  + [_layer_spec(w, layer // 2) for w in mix_w]
                  + [_layer_spec(w, layer) for w in ffn_w]),
        out_specs=[pl.BlockSpec((tm, d), lambda i: (jnp.maximum(i - 1, 0), 0)),
                   pl.BlockSpec((1, CONV_STATE, bw),
                                lambda i: (jnp.minimum(i, last) // tiles_per_seq, 0, 0))],
        out_shape=[jax.ShapeDtypeStruct((rows, d), F32),
                   jax.ShapeDtypeStruct((n_seq, CONV_STATE, bw), F32)],
        scratch_shapes=[pltpu.VMEM((HALO + tm, bw), F32),
                        pltpu.VMEM((tm, bw), F32),
                        pltpu.VMEM((tm, aw + bw), BF16),
                        pltpu.VMEM((2, tm, d), F32)],
        compiler_params=_params(1),
        name="even_layer",
    )(x, *mix_w, *ffn_w)


def _even_sample_body(x_ref, st_ref, gn_ref, win_ref, lng_ref, lnb_ref, wsd_ref, bsd_ref,
                      cw_ref, cb_ref, clg_ref, clb_ref, wout_ref,
                      out_ref, v_ref, nst_ref, *, aw, bw):
    x = x_ref[...]
    h = _rms(x, gn_ref[...]).astype(BF16)
    u = _gelu(_dot(h, win_ref[:, 0:aw]))
    v = _ln(_gelu(_dot(h, win_ref[:, aw:2 * aw])), lng_ref[...], lnb_ref[...])
    v_ref[...] = v
    s = v * wsd_ref[...] + bsd_ref[...]
    a_out = u * s

    zb = _dot(h, win_ref[:, 2 * aw:2 * aw + bw])
    zg = _dot(h, win_ref[:, 2 * aw + bw:])
    glu = zb * jax.nn.sigmoid(zg)

    c = glu * cw_ref[CONV_STATE:CONV_WIDTH, :] + cb_ref[...]
    for k in range(CONV_STATE):
        c = c + st_ref[k] * cw_ref[k:k + 1, :]
        if k > 0:
            nst_ref[k - 1] = st_ref[k]
    nst_ref[CONV_STATE - 1] = glu
    b_out = _silu(_ln(c, clg_ref[...], clb_ref[...]))

    ab = jnp.concatenate([a_out.astype(BF16), b_out.astype(BF16)], axis=-1)
    out_ref[...] = x + _dot(ab, wout_ref[...])


def _even_sample(x, state_t, layer, mix_w, *, bs):
    rows, d = x.shape
    aw = mix_w[2].shape[-1]
    bw = mix_w[8].shape[-1]
    row_spec = pl.BlockSpec((bs, d), lambda i: (i, 0))
    return pl.pallas_call(
        functools.partial(_even_sample_body, aw=aw, bw=bw),
        grid=(rows // bs,),
        in_specs=([row_spec,
                   pl.BlockSpec((None, CONV_STATE, bs, bw), lambda i: (layer // 2, 0, i, 0))]
                  + [_layer_spec(w, layer // 2) for w in mix_w]),
        out_specs=[row_spec, pl.BlockSpec((bs, aw), lambda i: (i, 0)),
                   pl.BlockSpec((CONV_STATE, bs, bw), lambda i: (0, i, 0))],
        out_shape=[jax.ShapeDtypeStruct((rows, d), F32),
                   jax.ShapeDtypeStruct((rows, aw), F32),
                   jax.ShapeDtypeStruct((CONV_STATE, rows, bw), F32)],
        compiler_params=_params(1),
        name="even_sample",
    )(x, state_t, *mix_w)


def _rope(t, cos, sin_lo, sin_hi):
    half = QK_ROPE // 2
    return (t * cos + pltpu.roll(t, LANES - half, 1) * sin_lo
            + pltpu.roll(t, half, 1) * sin_hi)


def _mla_proj_body(*refs, with_kv, q_lora, kv_lora):
    (x_ref, gn_ref, win_ref, qn_ref, kvn_ref, wuq_ref, wuk_ref, wuvt_ref,
     cos_ref, slo_ref, shi_ref) = refs[:11]
    if with_kv:
        q_ref, k_ref, vt_ref, ckv_ref, kpe_ref = refs[11:]
    else:
        q_ref, ckv_ref, kpe_ref = refs[11:]

    x = x_ref[...]
    h = _rms(x, gn_ref[...]).astype(BF16)
    z = _dot(h, win_ref[...])
    cq = _rms(z[:, 0:q_lora], qn_ref[...])
    ckv = _rms(z[:, q_lora:q_lora + kv_lora], kvn_ref[...])
    cos, slo, shi = cos_ref[...], slo_ref[...], shi_ref[...]
    kr = _rope(z[:, q_lora + kv_lora:], cos, slo, shi)
    ckv_ref[...] = ckv
    kpe_ref[...] = kr[:, 0:QK_ROPE]

    q = _dot(cq.astype(BF16), wuq_ref[...])
    for hh in range(MLA_HEADS):
        base = hh * QK_PAD
        q_ref[hh, :, 0:QK_NOPE] = (q[:, base:base + QK_NOPE] * Q_SCALE).astype(BF16)
        qp = _rope(q[:, base + QK_NOPE:base + QK_PAD], cos, slo, shi)
        q_ref[hh, :, QK_NOPE:QK_PAD] = (qp * Q_SCALE).astype(BF16)

    if with_kv:
        cb = ckv.astype(BF16)
        kn = _dot(cb, wuk_ref[...])
        vt = _dot_nt(wuvt_ref[...], cb)
        krb = kr.astype(BF16)
        for hh in range(MLA_HEADS):
            k_ref[hh, :, 0:QK_NOPE] = kn[:, hh * QK_NOPE:(hh + 1) * QK_NOPE].astype(BF16)
            k_ref[hh, :, QK_NOPE:QK_PAD] = krb
            vt_ref[hh, 0] = vt[hh * V_HEAD:(hh + 1) * V_HEAD, :].astype(BF16)


def _mla_proj(x, odd, proj_w, cos, slo, shi, *, tm, with_kv, tk=None):
    rows, d = x.shape
    q_lora = proj_w[2].shape[-1]
    kv_lora = proj_w[3].shape[-1]
    n_pos_tiles = cos.shape[0] // tm
    row_spec = pl.BlockSpec((tm, d), lambda i: (i, 0))
    pos_spec = pl.BlockSpec((tm, LANES), lambda i: (i % n_pos_tiles, 0))
    head_spec = pl.BlockSpec((MLA_HEADS, tm, QK_PAD), lambda i: (0, i, 0))
    out_specs = [head_spec]
    out_shape = [jax.ShapeDtypeStruct((MLA_HEADS, rows, QK_PAD), BF16)]
    if with_kv:
        per_tile = tk // tm
        out_specs += [head_spec,
                      pl.BlockSpec((MLA_HEADS, 1, V_HEAD, tm),
                                   lambda i: (0, i // per_tile, 0, i % per_tile))]
        out_shape += [jax.ShapeDtypeStruct((MLA_HEADS, rows, QK_PAD), BF16),
                      jax.ShapeDtypeStruct((MLA_HEADS, rows // tk, V_HEAD, tk), BF16)]
    out_specs += [pl.BlockSpec((tm, kv_lora), lambda i: (i, 0)),
                  pl.BlockSpec((tm, QK_ROPE), lambda i: (i, 0))]
    out_shape += [jax.ShapeDtypeStruct((rows, kv_lora), F32),
                  jax.ShapeDtypeStruct((rows, QK_ROPE), F32)]
    return pl.pallas_call(
        functools.partial(_mla_proj_body, with_kv=with_kv, q_lora=q_lora, kv_lora=kv_lora),
        grid=(rows // tm,),
        in_specs=[row_spec] + [_layer_spec(w, odd) for w in proj_w] + [pos_spec] * 3,
        out_specs=out_specs,
        out_shape=out_shape,
        compiler_params=_params(1),
        name="mla_proj",
    )(x, *proj_w, cos, slo, shi)


def _flash_body(q_ref, k_ref, vt_ref, o_ref, st_ref, mt_ref, m_ref, l_ref, acc_ref, *, tq):
    nq = q_ref.shape[1] // tq

    def tile(ref, t):
        return ref[0, pl.ds(pl.multiple_of(t * tq, tq), tq), :]

    def put_scores(st):
        st_ref[...] = st
        mt_ref[...] = jnp.max(st, axis=0, keepdims=True)

    def consume(j, st, mt):
        m_prev = m_ref[...]
        m_new = jnp.maximum(m_prev, mt)
        alpha = jnp.exp2(m_prev - m_new)
        l_new = alpha * l_ref[...]
        pv = None
        for c in range(0, tq, KV_CHUNK):
            pt = jnp.exp2(st[c:c + KV_CHUNK] - m_new)
            l_new = l_new + jnp.sum(pt, axis=0, keepdims=True)
            d = _dot(vt_ref[0, j][:, c:c + KV_CHUNK], pt.astype(BF16))
            pv = d if pv is None else pv + d
        l_ref[...] = l_new
        acc_ref[...] = alpha * acc_ref[...] + pv
        m_ref[...] = m_new

    put_scores(_dot_nt(tile(k_ref, 0), tile(q_ref, 0)))

    def q_tile(qi, carry):
        q = tile(q_ref, qi)
        m_ref[...] = jnp.full(m_ref.shape, -jnp.inf, F32)
        l_ref[...] = jnp.zeros(l_ref.shape, F32)
        acc_ref[...] = jnp.zeros(acc_ref.shape, F32)

        def kv_tile(j, c):
            st = st_ref[...]
            mt = mt_ref[...]
            st_next = _dot_nt(tile(k_ref, j + 1), q)
            consume(j, st, mt)
            put_scores(st_next)
            return c

        lax.fori_loop(0, qi, kv_tile, 0)
        key = lax.broadcasted_iota(jnp.int32, (tq, tq), 0)
        qry = lax.broadcasted_iota(jnp.int32, (tq, tq), 1)
        st = jnp.where(key <= qry, st_ref[...], -jnp.inf)
        st_next = _dot_nt(tile(k_ref, 0), tile(q_ref, jnp.minimum(qi + 1, nq - 1)))
        consume(qi, st, jnp.max(st, axis=0, keepdims=True))
        put_scores(st_next)
        o_ref[pl.ds(pl.multiple_of(qi * tq, tq), tq), :] = (acc_ref[...] / l_ref[...]).T.astype(BF16)
        return carry

    lax.fori_loop(0, nq, q_tile, 0)


def _flash(q, k, vt, n_seq, *, tq):
    heads, rows, _ = q.shape
    seq = rows // n_seq
    qk_spec = pl.BlockSpec((1, seq, QK_PAD), lambda b, h: (h, b, 0))
    return pl.pallas_call(
        functools.partial(_flash_body, tq=tq),
        grid=(n_seq, heads),
        in_specs=[qk_spec, qk_spec,
                  pl.BlockSpec((1, seq // tq, V_HEAD, tq), lambda b, h: (h, b, 0, 0))],
        out_specs=pl.BlockSpec((seq, V_HEAD), lambda b, h: (b, h)),
        out_shape=jax.ShapeDtypeStruct((rows, heads * V_HEAD), BF16),
        scratch_shapes=[pltpu.VMEM((tq, tq), F32), pltpu.VMEM((1, tq), F32),
                        pltpu.VMEM((1, tq), F32), pltpu.VMEM((1, tq), F32),
                        pltpu.VMEM((V_HEAD, tq), F32)],
        compiler_params=_params(2),
        name="flash",
    )(q, k, vt)


def _absorb_q_body(q_ref, wukt_ref, o_ref):
    o_ref[0] = _dot(q_ref[0, :, 0:QK_NOPE], wukt_ref[0]).astype(BF16)


def _absorb_q(q, odd, wukt):
    heads, rows, _ = q.shape
    kv_lora = wukt.shape[-1]
    return pl.pallas_call(
        _absorb_q_body,
        grid=(heads,),
        in_specs=[pl.BlockSpec((1, rows, QK_PAD), lambda h: (h, 0, 0)),
                  pl.BlockSpec((None, 1, QK_NOPE, kv_lora), lambda h: (odd, h, 0, 0))],
        out_specs=pl.BlockSpec((1, rows, kv_lora), lambda h: (h, 0, 0)),
        out_shape=jax.ShapeDtypeStruct((heads, rows, kv_lora), BF16),
        compiler_params=_params(1),
        name="absorb_q",
    )(q, wukt)


def _page_copy(pt_ref, ckv_hbm, kpe_hbm, ckv_buf, kpe_buf, sem, layer, seq, slot, j, page):
    pid = pt_ref[seq, j]
    keys = pl.ds(j * page, page)
    return (pltpu.make_async_copy(ckv_hbm.at[layer, pid], ckv_buf.at[slot, keys, :], sem.at[0, slot]),
            pltpu.make_async_copy(kpe_hbm.at[layer, pid], kpe_buf.at[slot, :, keys], sem.at[1, slot]))


def _decode_body(pt_ref, ql_ref, qp_ref, cn_ref, kn_ref, ckv_hbm, kpe_hbm, o_ref,
                 ckv_buf, kpe_buf, s_ref, sem, *, layer, n_pages, page, kc):
    b = pl.program_id(0)
    nb = pl.num_programs(0)
    slot = b % 2
    copy = functools.partial(_page_copy, pt_ref, ckv_hbm, kpe_hbm, ckv_buf, kpe_buf, sem, layer)

    def start_all(seq, slt):
        for j in range(n_pages):
            c0, c1 = copy(seq, slt, j, page)
            c0.start()
            c1.start()

    @pl.when(b == 0)
    def _():
        start_all(0, 0)

    @pl.when(b + 1 < nb)
    def _():
        start_all(b + 1, 1 - slot)

    def wait_one(j, carry):
        c0, c1 = copy(b, slot, j, page)
        c0.wait()
        c1.wait()
        return carry

    lax.fori_loop(0, n_pages, wait_one, 0)

    ql = ql_ref[0]
    qp = qp_ref[0]
    past = n_pages * page
    for c in range(past // kc):
        keys = pl.ds(c * kc, kc)
        s_ref[:, c * kc:(c + 1) * kc] = (
            _dot_nt(ql, ckv_buf[slot, keys, :].astype(BF16))
            + _dot(qp, kpe_buf[slot, :, keys].astype(BF16)))

    cn = cn_ref[0]
    kn = kn_ref[0]
    s_new = (jnp.sum(ql.astype(F32) * cn, axis=-1, keepdims=True)
             + jnp.sum(qp.astype(F32) * kn, axis=-1, keepdims=True))
    s = s_ref[...]
    m = jnp.maximum(jnp.max(s, axis=-1, keepdims=True), s_new)
    p = jnp.exp2(s - m)
    p_new = jnp.exp2(s_new - m)
    denom = jnp.sum(p, axis=-1, keepdims=True) + p_new
    pb = p.astype(BF16)
    acc = p_new * cn
    for c in range(past // kc):
        keys = pl.ds(c * kc, kc)
        acc = acc + _dot(pb[:, c * kc:(c + 1) * kc], ckv_buf[slot, keys, :].astype(BF16))
    o_ref[0] = acc / denom


def _decode_attn(page_table, q_lat, q_pe, ckv_new, kpe_new, cache_ckv, cache_kpe_t, *, layer):
    n_seq, heads, kv_lora = q_lat.shape
    n_pages = page_table.shape[1]
    page = cache_ckv.shape[2]
    past = n_pages * page
    kc = min(past, 1024)
    grid_spec = pltpu.PrefetchScalarGridSpec(
        num_scalar_prefetch=1,
        grid=(n_seq,),
        in_specs=[pl.BlockSpec((1, heads, kv_lora), lambda b, pt: (b, 0, 0)),
                  pl.BlockSpec((1, heads, QK_ROPE), lambda b, pt: (b, 0, 0)),
                  pl.BlockSpec((1, 1, kv_lora), lambda b, pt: (b, 0, 0)),
                  pl.BlockSpec((1, 1, QK_ROPE), lambda b, pt: (b, 0, 0)),
                  pl.BlockSpec(memory_space=pl.ANY),
                  pl.BlockSpec(memory_space=pl.ANY)],
        out_specs=pl.BlockSpec((1, heads, kv_lora), lambda b, pt: (b, 0, 0)),
        scratch_shapes=[pltpu.VMEM((2, past, kv_lora), F32),
                        pltpu.VMEM((2, QK_ROPE, past), F32),
                        pltpu.VMEM((heads, past), F32),
                        pltpu.SemaphoreType.DMA((2, 2))],
    )
    return pl.pallas_call(
        functools.partial(_decode_body, layer=layer, n_pages=n_pages, page=page, kc=kc),
        grid_spec=grid_spec,
        out_shape=jax.ShapeDtypeStruct((n_seq, heads, kv_lora), F32),
        compiler_params=_params(1),
        name="decode_attn",
    )(page_table, q_lat, q_pe, ckv_new, kpe_new, cache_ckv, cache_kpe_t)


def _value_up_body(ol_ref, wuv_ref, o_ref):
    o_ref[...] = _dot(ol_ref[0].astype(BF16), wuv_ref[0]).astype(BF16)


def _value_up(o_lat, odd, wuv_h):
    heads, rows, kv_lora = o_lat.shape
    return pl.pallas_call(
        _value_up_body,
        grid=(heads,),
        in_specs=[pl.BlockSpec((1, rows, kv_lora), lambda h: (h, 0, 0)),
                  pl.BlockSpec((None, 1, kv_lora, V_HEAD), lambda h: (odd, h, 0, 0))],
        out_specs=pl.BlockSpec((rows, V_HEAD), lambda h: (0, h)),
        out_shape=jax.ShapeDtypeStruct((rows, heads * V_HEAD), BF16),
        compiler_params=_params(1),
        name="value_up",
    )(o_lat, wuv_h)


def _rope_tables(pos):
    half = QK_ROPE // 2
    inv = ROPE_THETA ** (-jnp.arange(half, dtype=F32) / half)
    ang = pos.astype(F32)[:, None] * inv[None, :]
    c, s = jnp.cos(ang), jnp.sin(ang)
    z = jnp.zeros_like(c)
    zz = jnp.zeros((pos.shape[0], LANES - QK_ROPE), F32)
    cos = jnp.concatenate([c, c, zz], axis=1)
    sin_lo = jnp.concatenate([-s, z, zz], axis=1)
    sin_hi = jnp.concatenate([z, s, zz], axis=1)
    return cos, sin_lo, sin_hi


def _row(v):
    return v.reshape(1, -1)


def kernel(x_prompt, x_sample, cache_ckv, cache_kpe, page_table, state_conv, norm_mix, norm_ffn, norm_final, w_in_ab, gmlp_ln_g, gmlp_ln_b, w_spatial, b_spatial, conv_w, conv_b, conv_ln_g, conv_ln_b, w_out_ab, w_in_c, q_norm, kv_norm, w_uq, w_uk, w_uv, w_out_c, w_gate, w_up, w_down):
    n_seq, seq, d = x_prompt.shape
    n_dec, dec_seq, _ = x_sample.shape
    assert dec_seq == 1, "sample group handles one new token per sequence"
    depth = norm_mix.shape[0]
    aw = gmlp_ln_g.shape[1]
    q_lora = q_norm.shape[1]
    kv_lora = kv_norm.shape[1]
    past = page_table.shape[1] * cache_ckv.shape[2]
    assert past % CHUNK == 0 and seq % CHUNK == 0

    tm = min(256, seq)
    tf = min(512, seq)
    tp = min(512, seq)
    tq = min(1024, seq)
    bs = min(32, n_dec)

    xp = x_prompt.reshape(n_seq * seq, d)
    xs = x_sample.reshape(n_dec, d)
    cache_kpe_t = jnp.swapaxes(cache_kpe, 2, 3)

    tril = jnp.tril(jnp.ones((CHUNK, CHUNK), dtype=bool))
    cos_p, slo_p, shi_p = _rope_tables(jnp.arange(seq))
    cos_s, slo_s, shi_s = _rope_tables(jnp.full((n_dec,), past))

    n_even, n_odd = w_in_ab.shape[0], w_in_c.shape[0]
    gw = aw // A_GROUPS

    def rows3(p):
        return p[:, None, :]

    ffn_w = (rows3(norm_ffn), w_gate.astype(BF16), w_up.astype(BF16), w_down.astype(BF16))
    ws_tril = jnp.where(tril[None, None], w_spatial, 0.0).astype(BF16)
    bs_full = jnp.repeat(jnp.swapaxes(b_spatial, 1, 2), gw, axis=2)
    mix_tail = (conv_w, rows3(conv_b), rows3(conv_ln_g), rows3(conv_ln_b), w_out_ab.astype(BF16))
    mix_head = (rows3(norm_mix[0::2]), w_in_ab.astype(BF16), rows3(gmlp_ln_g), rows3(gmlp_ln_b))
    mix_w = mix_head + (ws_tril, bs_full) + mix_tail
    mix_w_s = mix_head + (rows3(jnp.repeat(w_spatial[:, :, 0, 0], gw, axis=1)),
                          rows3(jnp.repeat(b_spatial[:, :, 0], gw, axis=1))) + mix_tail
    state_t = jnp.swapaxes(state_conv, 1, 2)

    win_c = jnp.pad(w_in_c, ((0, 0), (0, 0), (0, LANES - QK_ROPE))).astype(BF16)
    wq = w_uq.reshape(n_odd, q_lora, MLA_HEADS, QK_NOPE + QK_ROPE)
    wq = jnp.pad(wq, ((0, 0), (0, 0), (0, 0), (0, QK_PAD - QK_NOPE - QK_ROPE)))
    wq = wq.reshape(n_odd, q_lora, MLA_HEADS * QK_PAD).astype(BF16)
    wuk = w_uk.reshape(n_odd, kv_lora, MLA_HEADS * QK_NOPE).astype(BF16)
    wuvt = jnp.swapaxes(w_uv.reshape(n_odd, kv_lora, MLA_HEADS * V_HEAD), 1, 2).astype(BF16)
    wukt = jnp.transpose(w_uk, (0, 2, 3, 1)).astype(BF16)
    wuv_h = jnp.transpose(w_uv, (0, 2, 1, 3)).astype(BF16)
    woc = w_out_c.astype(BF16)
    proj_w = (rows3(norm_mix[1::2]), win_c, rows3(q_norm), rows3(kv_norm), wq, wuk, wuvt)

    ckv_p, kpe_p, ckv_s, kpe_s, conv_p, conv_s, v_s = [], [], [], [], [], [], []
    for layer in range(depth):
        last = layer == depth - 1
        nf = _row(norm_final) if last else None
        if layer % 2 == 0:
            xs, vrows, buf_s = _even_sample(xs, state_t, layer, mix_w_s, bs=bs)
            (xs,) = _ffn(xs, layer, ffn_w, tm=n_dec)
            xp, buf_p = _even_layer_prompt(xp, n_seq, layer, mix_w, ffn_w, tm=tm)
            conv_p.append(buf_p)
            conv_s.append(buf_s)
            v_s.append(vrows.reshape(n_dec, 1, aw))
        else:
            o = layer // 2
            q, k, vt, ckv, kpe = _mla_proj(xp, o, proj_w, cos_p, slo_p, shi_p,
                                           tm=tp, with_kv=True, tk=tq)
            attn_p = _flash(q, k, vt, n_seq, tq=tq)
            ckv_p.append(ckv.reshape(n_seq, seq, kv_lora))
            kpe_p.append(kpe.reshape(n_seq, seq, QK_ROPE))

            qs, ckv_n, kpe_n = _mla_proj(xs, o, proj_w, cos_s, slo_s, shi_s, tm=n_dec, with_kv=False)
            q_lat = jnp.transpose(_absorb_q(qs, o, wukt), (1, 0, 2))
            q_pe = jnp.transpose(qs[:, :, QK_NOPE:QK_NOPE + QK_ROPE], (1, 0, 2))
            o_lat = _decode_attn(page_table, q_lat, q_pe,
                                 ckv_n.reshape(n_dec, 1, kv_lora), kpe_n.reshape(n_dec, 1, QK_ROPE),
                                 cache_ckv, cache_kpe_t, layer=o)
            attn_s = _value_up(jnp.transpose(o_lat, (1, 0, 2)), o, wuv_h)
            ckv_s.append(ckv_n.reshape(n_dec, 1, kv_lora))
            kpe_s.append(kpe_n.reshape(n_dec, 1, QK_ROPE))

            res_p = _ffn(xp, layer, ffn_w, tm=tf, o=attn_p, wo=(woc, o), gf=nf, write_x=not last)
            res_s = _ffn(xs, layer, ffn_w, tm=n_dec, o=attn_s, wo=(woc, o), gf=nf, write_x=not last)
            xp, xs = res_p[0], res_s[0]
    assert depth % 2 == 0, "the final RMSNorm is fused into the last (odd) layer's FFN"

    y_prompt = xp.reshape(n_seq, seq, d)
    y_sample = xs.reshape(n_dec, 1, d)
    return (y_prompt, y_sample, jnp.stack(ckv_p), jnp.stack(kpe_p), jnp.stack(ckv_s),
            jnp.stack(kpe_s), jnp.stack(conv_p), jnp.swapaxes(jnp.stack(conv_s), 1, 2),
            jnp.stack(v_s))
```

```python
import functools
import math

import jax
import jax.numpy as jnp
from jax import lax
from jax.experimental import pallas as pl
from jax.experimental.pallas import tpu as pltpu

F32 = jnp.float32
BF16 = jnp.bfloat16

EPS = 1e-6
CHUNK = 128
A_GROUPS = 4
CONV_WIDTH = 31
CONV_STATE = CONV_WIDTH - 1
MLA_HEADS = 8
QK_NOPE = 128
QK_ROPE = 64
V_HEAD = 128
ROPE_THETA = 10000.0
ATTN_SCALE = (QK_NOPE + QK_ROPE) ** -0.5
Q_SCALE = ATTN_SCALE * math.log2(math.e)

LANES = 128
SUBLANES = 8
HALO = 32
CONV_ROW_BLOCK = 64
KV_CHUNK = 256
QK_PAD = 2 * LANES
VMEM_LIMIT = 56 * 1024 * 1024


def _dot(a, b):
    return jnp.dot(a, b, preferred_element_type=F32)


def _dot_nt(a, b):
    return lax.dot_general(a, b, (((1,), (1,)), ((), ())), preferred_element_type=F32)


def _rms(x, g):
    return x * lax.rsqrt(jnp.mean(x * x, axis=-1, keepdims=True) + EPS) * g


def _ln(x, g, b):
    mu = jnp.mean(x, axis=-1, keepdims=True)
    xc = x - mu
    var = jnp.mean(xc * xc, axis=-1, keepdims=True)
    return xc * lax.rsqrt(var + EPS) * g + b


def _gelu(x):
    return 0.5 * x * (1.0 + jnp.tanh(math.sqrt(2.0 / math.pi) * (x + 0.044715 * (x * x * x))))


def _silu(x):
    return x * jax.nn.sigmoid(x)


def _const_spec(shape):
    nd = len(shape)
    return pl.BlockSpec(shape, lambda *_: (0,) * nd, pipeline_mode=pl.Buffered(1))


def _layer_spec(stacked, layer):
    tail = stacked.shape[1:]
    return pl.BlockSpec((None,) + tail, lambda *_: (layer,) + (0,) * len(tail),
                        pipeline_mode=pl.Buffered(1))


def _params(n_axes):
    return pltpu.CompilerParams(dimension_semantics=("arbitrary",) * n_axes,
                                vmem_limit_bytes=VMEM_LIMIT)


def _swiglu(x, gn_ref, wg_ref, wu_ref, wd_ref):
    h = _rms(x, gn_ref[...]).astype(BF16)
    a = (_silu(_dot(h, wg_ref[...])) * _dot(h, wu_ref[...])).astype(BF16)
    return x + _dot(a, wd_ref[...])


def _ffn_body(*refs, has_proj, has_final, write_x):
    it = iter(refs)
    x_ref = next(it)
    if has_proj:
        o_ref = next(it)
        wo_ref = next(it)
    gn_ref, wg_ref, wu_ref, wd_ref = next(it), next(it), next(it), next(it)
    if has_final:
        gf_ref = next(it)
    if write_x:
        out_ref = next(it)
    if has_final:
        yf_ref = next(it)

    x = x_ref[...]
    if has_proj:
        x = x + _dot(o_ref[...], wo_ref[...])
    xn = _swiglu(x, gn_ref, wg_ref, wu_ref, wd_ref)
    if write_x:
        out_ref[...] = xn
    if has_final:
        yf_ref[...] = _rms(xn, gf_ref[...])


def _ffn(x, layer, ffn_w, *, tm, o=None, wo=None, gf=None, write_x=True):
    rows, d = x.shape
    has_proj = o is not None
    has_final = gf is not None
    row_spec = pl.BlockSpec((tm, d), lambda i: (i, 0))
    args, specs = [x], [row_spec]
    if has_proj:
        args += [o, wo[0]]
        specs += [pl.BlockSpec((tm, o.shape[1]), lambda i: (i, 0)), _layer_spec(*wo)]
    args += list(ffn_w)
    specs += [_layer_spec(w, layer) for w in ffn_w]
    if has_final:
        args.append(gf)
        specs.append(_const_spec(gf.shape))
    out_shape, out_specs = [], []
    if write_x:
        out_shape.append(jax.ShapeDtypeStruct((rows, d), F32))
        out_specs.append(row_spec)
    if has_final:
        out_shape.append(jax.ShapeDtypeStruct((rows, d), F32))
        out_specs.append(row_spec)
    return pl.pallas_call(
        functools.partial(_ffn_body, has_proj=has_proj, has_final=has_final, write_x=write_x),
        grid=(rows // tm,),
        in_specs=specs,
        out_specs=out_specs,
        out_shape=out_shape,
        compiler_params=_params(1),
        name="ffn",
    )(*args)


def _even_layer_body(x_ref, gn_ref, win_ref, lng_ref, lnb_ref, ws_ref, bs_ref,
                     cw_ref, cb_ref, clg_ref, clb_ref, wout_ref,
                     gf_ref, wg_ref, wu_ref, wd_ref,
                     out_ref, conv_ref, xe_ref, cv_ref, ab_ref, x1_ref, *, tm, tiles_per_seq, aw, bw):
    i = pl.program_id(0)

    @pl.when(i == 0)
    def _():
        x1_ref[1] = jnp.zeros(x1_ref.shape[1:], F32)

    @pl.when(i % tiles_per_seq == 0)
    def _():
        xe_ref[0:HALO, :] = jnp.zeros((HALO, bw), F32)

    x_prev = x1_ref[(i + 1) % 2]
    x = x_ref[...]
    h = _rms(x, gn_ref[...]).astype(BF16)

    u = _gelu(_dot(h, win_ref[:, 0:aw]))
    v = _ln(_gelu(_dot(h, win_ref[:, aw:2 * aw])), lng_ref[...], lnb_ref[...])
    vb = v.astype(BF16)
    gw = aw // A_GROUPS
    for c in range(tm // CHUNK):
        r = slice(c * CHUNK, (c + 1) * CHUNK)
        for g in range(A_GROUPS):
            cs = slice(g * gw, (g + 1) * gw)
            s = _dot(ws_ref[g], vb[r, cs]) + bs_ref[:, cs]
            ab_ref[r, cs] = (u[r, cs] * s).astype(BF16)

    zb = _dot(h, win_ref[:, 2 * aw:2 * aw + bw])
    zg = _dot(h, win_ref[:, 2 * aw + bw:])
    xe_ref[HALO:HALO + tm, :] = zb * jax.nn.sigmoid(zg)

    off = HALO - CONV_STATE
    rb = min(tm, CONV_ROW_BLOCK)
    for r0 in range(0, tm, rb):
        for c0 in range(0, bw, LANES):
            cs = slice(c0, c0 + LANES)
            acc = jnp.broadcast_to(cb_ref[:, cs], (rb, LANES))
            for r in range(SUBLANES):
                taps = [a for a in range(HALO // SUBLANES + 1)
                        if 0 <= a * SUBLANES + r - off < CONV_WIDTH]
                lo = r0 + taps[0] * SUBLANES
                span = rb + (taps[-1] - taps[0]) * SUBLANES
                if r == 0:
                    win = xe_ref[lo:lo + span, cs]
                else:
                    win = pltpu.roll(xe_ref[lo:lo + span + SUBLANES, cs], span + SUBLANES - r, 0)
                for a in taps:
                    k = a * SUBLANES + r - off
                    sh = (a - taps[0]) * SUBLANES
                    acc = acc + cw_ref[k:k + 1, cs] * win[sh:sh + rb]
            cv_ref[r0:r0 + rb, cs] = acc
    ab_ref[:, aw:aw + bw] = _silu(_ln(cv_ref[...], clg_ref[...], clb_ref[...])).astype(BF16)

    conv_ref[0] = xe_ref[tm + off:tm + HALO, :]
    xe_ref[0:HALO, :] = xe_ref[tm:tm + HALO, :]
    x1 = x + _dot(ab_ref[...], wout_ref[...])

    out_ref[...] = _swiglu(x_prev, gf_ref, wg_ref, wu_ref, wd_ref)
    x1_ref[i % 2] = x1


def _even_layer_prompt(x, n_seq, layer, mix_w, ffn_w, *, tm):
    rows, d = x.shape
    aw = mix_w[2].shape[-1]
    bw = mix_w[8].shape[-1]
    n_tiles = rows // tm
    tiles_per_seq = n_tiles // n_seq
    last = n_tiles - 1
    return pl.pallas_call(
        functools.partial(_even_layer_body, tm=tm, tiles_per_seq=tiles_per_seq, aw=aw, bw=bw),
        grid=(n_tiles + 1,),
        in_specs=([pl.BlockSpec((tm, d), lambda i: (jnp.minimum(i, last), 0))]
                  + [_layer_spec(w, layer // 2) for w in mix_w]
                  + [_layer_spec(w, layer) for w in ffn_w]),
        out_specs=[pl.BlockSpec((tm, d), lambda i: (jnp.maximum(i - 1, 0), 0)),
                   pl.BlockSpec((1, CONV_STATE, bw),
                                lambda i: (jnp.minimum(i, last) // tiles_per_seq, 0, 0))],
        out_shape=[jax.ShapeDtypeStruct((rows, d), F32),
                   jax.ShapeDtypeStruct((n_seq, CONV_STATE, bw), F32)],
        scratch_shapes=[pltpu.VMEM((HALO + tm, bw), F32),
                        pltpu.VMEM((tm, bw), F32),
                        pltpu.VMEM((tm, aw + bw), BF16),
                        pltpu.VMEM((2, tm, d), F32)],
        compiler_params=_params(1),
        name="even_layer",
    )(x, *mix_w, *ffn_w)


def _even_sample_body(x_ref, st_ref, gn_ref, win_ref, lng_ref, lnb_ref, wsd_ref, bsd_ref,
                      cw_ref, cb_ref, clg_ref, clb_ref, wout_ref,
                      out_ref, v_ref, nst_ref, *, aw, bw):
    x = x_ref[...]
    h = _rms(x, gn_ref[...]).astype(BF16)
    u = _gelu(_dot(h, win_ref[:, 0:aw]))
    v = _ln(_gelu(_dot(h, win_ref[:, aw:2 * aw])), lng_ref[...], lnb_ref[...])
    v_ref[...] = v
    s = v * wsd_ref[...] + bsd_ref[...]
    a_out = u * s

    zb = _dot(h, win_ref[:, 2 * aw:2 * aw + bw])
    zg = _dot(h, win_ref[:, 2 * aw + bw:])
    glu = zb * jax.nn.sigmoid(zg)

    c = glu * cw_ref[CONV_STATE:CONV_WIDTH, :] + cb_ref[...]
    for k in range(CONV_STATE):
        c = c + st_ref[k] * cw_ref[k:k + 1, :]
        if k > 0:
            nst_ref[k - 1] = st_ref[k]
    nst_ref[CONV_STATE - 1] = glu
    b_out = _silu(_ln(c, clg_ref[...], clb_ref[...]))

    ab = jnp.concatenate([a_out.astype(BF16), b_out.astype(BF16)], axis=-1)
    out_ref[...] = x + _dot(ab, wout_ref[...])


def _even_sample(x, state_t, layer, mix_w, *, bs):
    rows, d = x.shape
    aw = mix_w[2].shape[-1]
    bw = mix_w[8].shape[-1]
    row_spec = pl.BlockSpec((bs, d), lambda i: (i, 0))
    return pl.pallas_call(
        functools.partial(_even_sample_body, aw=aw, bw=bw),
        grid=(rows // bs,),
        in_specs=([row_spec,
                   pl.BlockSpec((None, CONV_STATE, bs, bw), lambda i: (layer // 2, 0, i, 0))]
                  + [_layer_spec(w, layer // 2) for w in mix_w]),
        out_specs=[row_spec, pl.BlockSpec((bs, aw), lambda i: (i, 0)),
                   pl.BlockSpec((CONV_STATE, bs, bw), lambda i: (0, i, 0))],
        out_shape=[jax.ShapeDtypeStruct((rows, d), F32),
                   jax.ShapeDtypeStruct((rows, aw), F32),
                   jax.ShapeDtypeStruct((CONV_STATE, rows, bw), F32)],
        compiler_params=_params(1),
        name="even_sample",
    )(x, state_t, *mix_w)


def _rope(t, cos, sin_lo, sin_hi):
    half = QK_ROPE // 2
    return (t * cos + pltpu.roll(t, LANES - half, 1) * sin_lo
            + pltpu.roll(t, half, 1) * sin_hi)


def _mla_proj_body(*refs, with_kv, q_lora, kv_lora):
    (x_ref, gn_ref, win_ref, qn_ref, kvn_ref, wuq_ref, wuk_ref, wuvt_ref,
     cos_ref, slo_ref, shi_ref) = refs[:11]
    if with_kv:
        q_ref, k_ref, vt_ref, ckv_ref, kpe_ref = refs[11:]
    else:
        q_ref, ckv_ref, kpe_ref = refs[11:]

    x = x_ref[...]
    h = _rms(x, gn_ref[...]).astype(BF16)
    z = _dot(h, win_ref[...])
    cq = _rms(z[:, 0:q_lora], qn_ref[...])
    ckv = _rms(z[:, q_lora:q_lora + kv_lora], kvn_ref[...])
    cos, slo, shi = cos_ref[...], slo_ref[...], shi_ref[...]
    kr = _rope(z[:, q_lora + kv_lora:], cos, slo, shi)
    ckv_ref[...] = ckv
    kpe_ref[...] = kr[:, 0:QK_ROPE]

    q = _dot(cq.astype(BF16), wuq_ref[...])
    for hh in range(MLA_HEADS):
        base = hh * QK_PAD
        q_ref[hh, :, 0:QK_NOPE] = (q[:, base:base + QK_NOPE] * Q_SCALE).astype(BF16)
        qp = _rope(q[:, base + QK_NOPE:base + QK_PAD], cos, slo, shi)
        q_ref[hh, :, QK_NOPE:QK_PAD] = (qp * Q_SCALE).astype(BF16)

    if with_kv:
        cb = ckv.astype(BF16)
        kn = _dot(cb, wuk_ref[...])
        vt = _dot_nt(wuvt_ref[...], cb)
        krb = kr.astype(BF16)
        for hh in range(MLA_HEADS):
            k_ref[hh, :, 0:QK_NOPE] = kn[:, hh * QK_NOPE:(hh + 1) * QK_NOPE].astype(BF16)
            k_ref[hh, :, QK_NOPE:QK_PAD] = krb
            vt_ref[hh, 0] = vt[hh * V_HEAD:(hh + 1) * V_HEAD, :].astype(BF16)


def _mla_proj(x, odd, proj_w, cos, slo, shi, *, tm, with_kv, tk=None):
    rows, d = x.shape
    q_lora = proj_w[2].shape[-1]
    kv_lora = proj_w[3].shape[-1]
    n_pos_tiles = cos.shape[0] // tm
    row_spec = pl.BlockSpec((tm, d), lambda i: (i, 0))
    pos_spec = pl.BlockSpec((tm, LANES), lambda i: (i % n_pos_tiles, 0))
    head_spec = pl.BlockSpec((MLA_HEADS, tm, QK_PAD), lambda i: (0, i, 0))
    out_specs = [head_spec]
    out_shape = [jax.ShapeDtypeStruct((MLA_HEADS, rows, QK_PAD), BF16)]
    if with_kv:
        per_tile = tk // tm
        out_specs += [head_spec,
                      pl.BlockSpec((MLA_HEADS, 1, V_HEAD, tm),
                                   lambda i: (0, i // per_tile, 0, i % per_tile))]
        out_shape += [jax.ShapeDtypeStruct((MLA_HEADS, rows, QK_PAD), BF16),
                      jax.ShapeDtypeStruct((MLA_HEADS, rows // tk, V_HEAD, tk), BF16)]
    out_specs += [pl.BlockSpec((tm, kv_lora), lambda i: (i, 0)),
                  pl.BlockSpec((tm, QK_ROPE), lambda i: (i, 0))]
    out_shape += [jax.ShapeDtypeStruct((rows, kv_lora), F32),
                  jax.ShapeDtypeStruct((rows, QK_ROPE), F32)]
    return pl.pallas_call(
        functools.partial(_mla_proj_body, with_kv=with_kv, q_lora=q_lora, kv_lora=kv_lora),
        grid=(rows // tm,),
        in_specs=[row_spec] + [_layer_spec(w, odd) for w in proj_w] + [pos_spec] * 3,
        out_specs=out_specs,
        out_shape=out_shape,
        compiler_params=_params(1),
        name="mla_proj",
    )(x, *proj_w, cos, slo, shi)


def _flash_body(q_ref, k_ref, vt_ref, o_ref, st_ref, mt_ref, m_ref, l_ref, acc_ref, *, tq):
    nq = q_ref.shape[1] // tq

    def tile(ref, t):
        return ref[0, pl.ds(pl.multiple_of(t * tq, tq), tq), :]

    def put_scores(st):
        st_ref[...] = st
        mt_ref[...] = jnp.max(st, axis=0, keepdims=True)

    def consume(j, st, mt):
        m_prev = m_ref[...]
        m_new = jnp.maximum(m_prev, mt)
        alpha = jnp.exp2(m_prev - m_new)
        l_new = alpha * l_ref[...]
        pv = None
        for c in range(0, tq, KV_CHUNK):
            pt = jnp.exp2(st[c:c + KV_CHUNK] - m_new)
            l_new = l_new + jnp.sum(pt, axis=0, keepdims=True)
            d = _dot(vt_ref[0, j][:, c:c + KV_CHUNK], pt.astype(BF16))
            pv = d if pv is None else pv + d
        l_ref[...] = l_new
        acc_ref[...] = alpha * acc_ref[...] + pv
        m_ref[...] = m_new

    def consume_diagonal(j, st):
        slabs = []
        mt = None
        for c in range(0, tq, KV_CHUNK):
            blk = st[c:c + KV_CHUNK, c:]
            key = lax.broadcasted_iota(jnp.int32, blk.shape, 0)
            qry = lax.broadcasted_iota(jnp.int32, blk.shape, 1)
            blk = jnp.where(key <= qry, blk, -jnp.inf)
            cm = jnp.max(blk, axis=0, keepdims=True)
            if c:
                cm = jnp.concatenate([jnp.full((1, c), -jnp.inf, F32), cm], axis=1)
            mt = cm if mt is None else jnp.maximum(mt, cm)
            slabs.append(blk)
        m_prev = m_ref[...]
        m_new = jnp.maximum(m_prev, mt)
        alpha = jnp.exp2(m_prev - m_new)
        l_new = alpha * l_ref[...]
        pv = None
        for c, blk in zip(range(0, tq, KV_CHUNK), slabs):
            pt = jnp.exp2(blk - m_new[:, c:])
            ls = jnp.sum(pt, axis=0, keepdims=True)
            d = _dot(vt_ref[0, j][:, c:c + KV_CHUNK], pt.astype(BF16))
            if c:
                ls = jnp.concatenate([jnp.zeros((1, c), F32), ls], axis=1)
                d = jnp.concatenate([jnp.zeros((V_HEAD, c), F32), d], axis=1)
            l_new = l_new + ls
            pv = d if pv is None else pv + d
        l_ref[...] = l_new
        acc_ref[...] = alpha * acc_ref[...] + pv
        m_ref[...] = m_new

    put_scores(_dot_nt(tile(k_ref, 0), tile(q_ref, 0)))

    def q_tile(qi, carry):
        q = tile(q_ref, qi)
        m_ref[...] = jnp.full(m_ref.shape, -jnp.inf, F32)
        l_ref[...] = jnp.zeros(l_ref.shape, F32)
        acc_ref[...] = jnp.zeros(acc_ref.shape, F32)

        def kv_tile(j, c):
            st = st_ref[...]
            mt = mt_ref[...]
            st_next = _dot_nt(tile(k_ref, j + 1), q)
            consume(j, st, mt)
            put_scores(st_next)
            return c

        lax.fori_loop(0, qi, kv_tile, 0)
        st = st_ref[...]
        st_next = _dot_nt(tile(k_ref, 0), tile(q_ref, jnp.minimum(qi + 1, nq - 1)))
        consume_diagonal(qi, st)
        put_scores(st_next)
        o_ref[pl.ds(pl.multiple_of(qi * tq, tq), tq), :] = (acc_ref[...] / l_ref[...]).T.astype(BF16)
        return carry

    lax.fori_loop(0, nq, q_tile, 0)


def _flash(q, k, vt, n_seq, *, tq):
    heads, rows, _ = q.shape
    seq = rows // n_seq
    qk_spec = pl.BlockSpec((1, seq, QK_PAD), lambda b, h: (h, b, 0))
    return pl.pallas_call(
        functools.partial(_flash_body, tq=tq),
        grid=(n_seq, heads),
        in_specs=[qk_spec, qk_spec,
                  pl.BlockSpec((1, seq // tq, V_HEAD, tq), lambda b, h: (h, b, 0, 0))],
        out_specs=pl.BlockSpec((seq, V_HEAD), lambda b, h: (b, h)),
        out_shape=jax.ShapeDtypeStruct((rows, heads * V_HEAD), BF16),
        scratch_shapes=[pltpu.VMEM((tq, tq), F32), pltpu.VMEM((1, tq), F32),
                        pltpu.VMEM((1, tq), F32), pltpu.VMEM((1, tq), F32),
                        pltpu.VMEM((V_HEAD, tq), F32)],
        compiler_params=_params(2),
        name="flash",
    )(q, k, vt)


def _absorb_q_body(q_ref, wukt_ref, o_ref):
    o_ref[0] = _dot(q_ref[0, :, 0:QK_NOPE], wukt_ref[0]).astype(BF16)


def _absorb_q(q, odd, wukt):
    heads, rows, _ = q.shape
    kv_lora = wukt.shape[-1]
    return pl.pallas_call(
        _absorb_q_body,
        grid=(heads,),
        in_specs=[pl.BlockSpec((1, rows, QK_PAD), lambda h: (h, 0, 0)),
                  pl.BlockSpec((None, 1, QK_NOPE, kv_lora), lambda h: (odd, h, 0, 0))],
        out_specs=pl.BlockSpec((1, rows, kv_lora), lambda h: (h, 0, 0)),
        out_shape=jax.ShapeDtypeStruct((heads, rows, kv_lora), BF16),
        compiler_params=_params(1),
        name="absorb_q",
    )(q, wukt)


def _page_copy(pt_ref, ckv_hbm, kpe_hbm, ckv_buf, kpe_buf, sem, layer, seq, slot, j, page):
    pid = pt_ref[seq, j]
    keys = pl.ds(j * page, page)
    return (pltpu.make_async_copy(ckv_hbm.at[layer, pid], ckv_buf.at[slot, keys, :], sem.at[0, slot]),
            pltpu.make_async_copy(kpe_hbm.at[layer, pid], kpe_buf.at[slot, :, keys], sem.at[1, slot]))


def _decode_body(pt_ref, ql_ref, qp_ref, cn_ref, kn_ref, ckv_hbm, kpe_hbm, o_ref,
                 ckv_buf, kpe_buf, s_ref, sem, *, layer, n_pages, page, kc):
    b = pl.program_id(0)
    nb = pl.num_programs(0)
    slot = b % 2
    copy = functools.partial(_page_copy, pt_ref, ckv_hbm, kpe_hbm, ckv_buf, kpe_buf, sem, layer)

    def start_all(seq, slt):
        for j in range(n_pages):
            c0, c1 = copy(seq, slt, j, page)
            c0.start()
            c1.start()

    @pl.when(b == 0)
    def _():
        start_all(0, 0)

    @pl.when(b + 1 < nb)
    def _():
        start_all(b + 1, 1 - slot)

    def wait_one(j, carry):
        c0, c1 = copy(b, slot, j, page)
        c0.wait()
        c1.wait()
        return carry

    lax.fori_loop(0, n_pages, wait_one, 0)

    ql = ql_ref[0]
    qp = qp_ref[0]
    past = n_pages * page
    for c in range(past // kc):
        keys = pl.ds(c * kc, kc)
        s_ref[:, c * kc:(c + 1) * kc] = (
            _dot_nt(ql, ckv_buf[slot, keys, :].astype(BF16))
            + _dot(qp, kpe_buf[slot, :, keys].astype(BF16)))

    cn = cn_ref[0]
    kn = kn_ref[0]
    s_new = (jnp.sum(ql.astype(F32) * cn, axis=-1, keepdims=True)
             + jnp.sum(qp.astype(F32) * kn, axis=-1, keepdims=True))
    s = s_ref[...]
    m = jnp.maximum(jnp.max(s, axis=-1, keepdims=True), s_new)
    p = jnp.exp2(s - m)
    p_new = jnp.exp2(s_new - m)
    denom = jnp.sum(p, axis=-1, keepdims=True) + p_new
    pb = p.astype(BF16)
    acc = p_new * cn
    for c in range(past // kc):
        keys = pl.ds(c * kc, kc)
        acc = acc + _dot(pb[:, c * kc:(c + 1) * kc], ckv_buf[slot, keys, :].astype(BF16))
    o_ref[0] = acc / denom


def _decode_attn(page_table, q_lat, q_pe, ckv_new, kpe_new, cache_ckv, cache_kpe_t, *, layer):
    n_seq, heads, kv_lora = q_lat.shape
    n_pages = page_table.shape[1]
    page = cache_ckv.shape[2]
    past = n_pages * page
    kc = min(past, 1024)
    grid_spec = pltpu.PrefetchScalarGridSpec(
        num_scalar_prefetch=1,
        grid=(n_seq,),
        in_specs=[pl.BlockSpec((1, heads, kv_lora), lambda b, pt: (b, 0, 0)),
                  pl.BlockSpec((1, heads, QK_ROPE), lambda b, pt: (b, 0, 0)),
                  pl.BlockSpec((1, 1, kv_lora), lambda b, pt: (b, 0, 0)),
                  pl.BlockSpec((1, 1, QK_ROPE), lambda b, pt: (b, 0, 0)),
                  pl.BlockSpec(memory_space=pl.ANY),
                  pl.BlockSpec(memory_space=pl.ANY)],
        out_specs=pl.BlockSpec((1, heads, kv_lora), lambda b, pt: (b, 0, 0)),
        scratch_shapes=[pltpu.VMEM((2, past, kv_lora), F32),
                        pltpu.VMEM((2, QK_ROPE, past), F32),
                        pltpu.VMEM((heads, past), F32),
                        pltpu.SemaphoreType.DMA((2, 2))],
    )
    return pl.pallas_call(
        functools.partial(_decode_body, layer=layer, n_pages=n_pages, page=page, kc=kc),
        grid_spec=grid_spec,
        out_shape=jax.ShapeDtypeStruct((n_seq, heads, kv_lora), F32),
        compiler_params=_params(1),
        name="decode_attn",
    )(page_table, q_lat, q_pe, ckv_new, kpe_new, cache_ckv, cache_kpe_t)


def _value_up_body(ol_ref, wuv_ref, o_ref):
    o_ref[...] = _dot(ol_ref[0].astype(BF16), wuv_ref[0]).astype(BF16)


def _value_up(o_lat, odd, wuv_h):
    heads, rows, kv_lora = o_lat.shape
    return pl.pallas_call(
        _value_up_body,
        grid=(heads,),
        in_specs=[pl.BlockSpec((1, rows, kv_lora), lambda h: (h, 0, 0)),
                  pl.BlockSpec((None, 1, kv_lora, V_HEAD), lambda h: (odd, h, 0, 0))],
        out_specs=pl.BlockSpec((rows, V_HEAD), lambda h: (0, h)),
        out_shape=jax.ShapeDtypeStruct((rows, heads * V_HEAD), BF16),
        compiler_params=_params(1),
        name="value_up",
    )(o_lat, wuv_h)


def _rope_tables(pos):
    half = QK_ROPE // 2
    inv = ROPE_THETA ** (-jnp.arange(half, dtype=F32) / half)
    ang = pos.astype(F32)[:, None] * inv[None, :]
    c, s = jnp.cos(ang), jnp.sin(ang)
    z = jnp.zeros_like(c)
    zz = jnp.zeros((pos.shape[0], LANES - QK_ROPE), F32)
    cos = jnp.concatenate([c, c, zz], axis=1)
    sin_lo = jnp.concatenate([-s, z, zz], axis=1)
    sin_hi = jnp.concatenate([z, s, zz], axis=1)
    return cos, sin_lo, sin_hi


def _row(v):
    return v.reshape(1, -1)


def kernel(x_prompt, x_sample, cache_ckv, cache_kpe, page_table, state_conv, norm_mix, norm_ffn, norm_final, w_in_ab, gmlp_ln_g, gmlp_ln_b, w_spatial, b_spatial, conv_w, conv_b, conv_ln_g, conv_ln_b, w_out_ab, w_in_c, q_norm, kv_norm, w_uq, w_uk, w_uv, w_out_c, w_gate, w_up, w_down):
    n_seq, seq, d = x_prompt.shape
    n_dec, dec_seq, _ = x_sample.shape
    assert dec_seq == 1, "sample group handles one new token per sequence"
    depth = norm_mix.shape[0]
    aw = gmlp_ln_g.shape[1]
    q_lora = q_norm.shape[1]
    kv_lora = kv_norm.shape[1]
    past = page_table.shape[1] * cache_ckv.shape[2]
    assert past % CHUNK == 0 and seq % CHUNK == 0

    tm = min(256, seq)
    tf = min(512, seq)
    tp = min(512, seq)
    tq = min(1024, seq)
    bs = min(32, n_dec)

    xp = x_prompt.reshape(n_seq * seq, d)
    xs = x_sample.reshape(n_dec, d)
    cache_kpe_t = jnp.swapaxes(cache_kpe, 2, 3)

    tril = jnp.tril(jnp.ones((CHUNK, CHUNK), dtype=bool))
    cos_p, slo_p, shi_p = _rope_tables(jnp.arange(seq))
    cos_s, slo_s, shi_s = _rope_tables(jnp.full((n_dec,), past))

    n_even, n_odd = w_in_ab.shape[0], w_in_c.shape[0]
    gw = aw // A_GROUPS

    def rows3(p):
        return p[:, None, :]

    ffn_w = (rows3(norm_ffn), w_gate.astype(BF16), w_up.astype(BF16), w_down.astype(BF16))
    ws_tril = jnp.where(tril[None, None], w_spatial, 0.0).astype(BF16)
    bs_full = jnp.repeat(jnp.swapaxes(b_spatial, 1, 2), gw, axis=2)
    mix_tail = (conv_w, rows3(conv_b), rows3(conv_ln_g), rows3(conv_ln_b), w_out_ab.astype(BF16))
    mix_head = (rows3(norm_mix[0::2]), w_in_ab.astype(BF16), rows3(gmlp_ln_g), rows3(gmlp_ln_b))
    mix_w = mix_head + (ws_tril, bs_full) + mix_tail
    mix_w_s = mix_head + (rows3(jnp.repeat(w_spatial[:, :, 0, 0], gw, axis=1)),
                          rows3(jnp.repeat(b_spatial[:, :, 0], gw, axis=1))) + mix_tail
    state_t = jnp.swapaxes(state_conv, 1, 2)

    win_c = jnp.pad(w_in_c, ((0, 0), (0, 0), (0, LANES - QK_ROPE))).astype(BF16)
    wq = w_uq.reshape(n_odd, q_lora, MLA_HEADS, QK_NOPE + QK_ROPE)
    wq = jnp.pad(wq, ((0, 0), (0, 0), (0, 0), (0, QK_PAD - QK_NOPE - QK_ROPE)))
    wq = wq.reshape(n_odd, q_lora, MLA_HEADS * QK_PAD).astype(BF16)
    wuk = w_uk.reshape(n_odd, kv_lora, MLA_HEADS * QK_NOPE).astype(BF16)
    wuvt = jnp.swapaxes(w_uv.reshape(n_odd, kv_lora, MLA_HEADS * V_HEAD), 1, 2).astype(BF16)
    wukt = jnp.transpose(w_uk, (0, 2, 3, 1)).astype(BF16)
    wuv_h = jnp.transpose(w_uv, (0, 2, 1, 3)).astype(BF16)
    woc = w_out_c.astype(BF16)
    proj_w = (rows3(norm_mix[1::2]), win_c, rows3(q_norm), rows3(kv_norm), wq, wuk, wuvt)

    ckv_p, kpe_p, ckv_s, kpe_s, conv_p, conv_s, v_s = [], [], [], [], [], [], []
    for layer in range(depth):
        last = layer == depth - 1
        nf = _row(norm_final) if last else None
        if layer % 2 == 0:
            xs, vrows, buf_s = _even_sample(xs, state_t, layer, mix_w_s, bs=bs)
            (xs,) = _ffn(xs, layer, ffn_w, tm=n_dec)
            xp, buf_p = _even_layer_prompt(xp, n_seq, layer, mix_w, ffn_w, tm=tm)
            conv_p.append(buf_p)
            conv_s.append(buf_s)
            v_s.append(vrows.reshape(n_dec, 1, aw))
        else:
            o = layer // 2
            q, k, vt, ckv, kpe = _mla_proj(xp, o, proj_w, cos_p, slo_p, shi_p,
                                           tm=tp, with_kv=True, tk=tq)
            attn_p = _flash(q, k, vt, n_seq, tq=tq)
            ckv_p.append(ckv.reshape(n_seq, seq, kv_lora))
            kpe_p.append(kpe.reshape(n_seq, seq, QK_ROPE))

            qs, ckv_n, kpe_n = _mla_proj(xs, o, proj_w, cos_s, slo_s, shi_s, tm=n_dec, with_kv=False)
            q_lat = jnp.transpose(_absorb_q(qs, o, wukt), (1, 0, 2))
            q_pe = jnp.transpose(qs[:, :, QK_NOPE:QK_NOPE + QK_ROPE], (1, 0, 2))
            o_lat = _decode_attn(page_table, q_lat, q_pe,
                                 ckv_n.reshape(n_dec, 1, kv_lora), kpe_n.reshape(n_dec, 1, QK_ROPE),
                                 cache_ckv, cache_kpe_t, layer=o)
            attn_s = _value_up(jnp.transpose(o_lat, (1, 0, 2)), o, wuv_h)
            ckv_s.append(ckv_n.reshape(n_dec, 1, kv_lora))
            kpe_s.append(kpe_n.reshape(n_dec, 1, QK_ROPE))

            res_p = _ffn(xp, layer, ffn_w, tm=tf, o=attn_p, wo=(woc, o), gf=nf, write_x=not last)
            res_s = _ffn(xs, layer, ffn_w, tm=n_dec, o=attn_s, wo=(woc, o), gf=nf, write_x=not last)
            xp, xs = res_p[0], res_s[0]
    assert depth % 2 == 0, "the final RMSNorm is fused into the last (odd) layer's FFN"

    y_prompt = xp.reshape(n_seq, seq, d)
    y_sample = xs.reshape(n_dec, 1, d)
    return (y_prompt, y_sample, jnp.stack(ckv_p), jnp.stack(kpe_p), jnp.stack(ckv_s),
            jnp.stack(kpe_s), jnp.stack(conv_p), jnp.swapaxes(jnp.stack(conv_s), 1, 2),
            jnp.stack(v_s))
```
